```python
import math
import jax
import jax.numpy as jnp
from jax import lax
import numpy as np


D_MODEL = 2048
BATCH = 4
SEQ = 8192
DEPTH = 1

ATTN_HEADS = 16
ATTN_HEAD_DIM = D_MODEL // ATTN_HEADS
ATTN_WIDTH = ATTN_HEADS * ATTN_HEAD_DIM
DILATED_PATTERNS = ((128, 1), (512, 4), (2048, 16))
ATTN_BLOCK = 128

SSM_EXPAND = 2
SSM_INNER = SSM_EXPAND * D_MODEL
SSM_HEAD_DIM = 64
SSM_HEADS = SSM_INNER // SSM_HEAD_DIM
SSM_GROUPS = 8
SSM_STATE = 128
SSM_CONV = 4
SSM_CHUNK = 128
SSM_CONV_DIM = SSM_INNER + 2 * SSM_GROUPS * SSM_STATE

RMS_EPS = 1e-6
IN_SIZES = (ATTN_WIDTH, ATTN_WIDTH, ATTN_WIDTH, ATTN_WIDTH,
            SSM_INNER, SSM_CONV_DIM, SSM_HEADS, D_MODEL, D_MODEL)
N_IN = 4 * ATTN_WIDTH + SSM_INNER + SSM_CONV_DIM + SSM_HEADS + 2 * D_MODEL

kernel_name = 'hybrid_dilated_attn_ssd_block'


def rms_norm(x, w):
    xf = x.astype(jnp.float32)
    y = xf * lax.rsqrt(jnp.mean(xf * xf, axis=-1, keepdims=True) + RMS_EPS)
    return (y * w.astype(jnp.float32)).astype(x.dtype)


def alibi_slopes(n_heads):
    return jnp.asarray([2.0 ** (-8.0 * (h + 1) / n_heads) for h in range(n_heads)], jnp.float32)


def dilated_window_attention(q, k, v, window, dilation, slopes):
    b, s, h, e = q.shape
    sub_len = s // dilation
    span = window // dilation
    blk = ATTN_BLOCK
    nb = -(-sub_len // blk)
    padded = nb * blk

    def to_sub(t):
        t = t.reshape(b, sub_len, dilation, h, e).transpose(0, 2, 3, 1, 4)
        t = jnp.pad(t, ((0, 0), (0, 0), (0, 0), (0, padded - sub_len), (0, 0)))
        return t.reshape(b, dilation, h, nb, blk, e)

    def with_prev(t):
        prev = jnp.pad(t[:, :, :, :-1], ((0, 0), (0, 0), (0, 0), (1, 0), (0, 0), (0, 0)))
        return jnp.concatenate([prev, t], axis=4)

    qb = to_sub(q)
    kc = with_prev(to_sub(k))
    vc = with_prev(to_sub(v))
    scores = jnp.einsum('brhiqe,brhike->brhiqk', qb, kc).astype(jnp.float32) * (e ** -0.5)

    qi = jnp.arange(blk)[:, None]
    ki = jnp.arange(2 * blk)[None, :]
    dist = qi - ki + blk
    key_idx = jnp.arange(nb)[:, None, None] * blk - blk + ki
    valid = (dist >= 0) & (dist <= span) & (key_idx >= 0)
    alibi = -slopes[:, None, None, None] * (dist * dilation).astype(jnp.float32)
    scores = jnp.where(valid, scores + alibi, -jnp.inf)
    lse = jax.nn.logsumexp(scores, axis=-1)
    p = jnp.exp(scores - lse[..., None])
    o = jnp.einsum('brhiqk,brhike->brhiqe', p.astype(v.dtype), vc)

    o = o.reshape(b, dilation, h, padded, e)[:, :, :, :sub_len]
    o = o.transpose(0, 3, 1, 2, 4).reshape(b, s, h, e)
    lse = lse.reshape(b, dilation, h, padded)[:, :, :, :sub_len]
    lse = lse.transpose(0, 3, 1, 2).reshape(b, s, h)
    return o, lse


def causal_depthwise_conv(x, w, bias):
    c = x.shape[-1]
    y = lax.conv_general_dilated(x, w[:, None, :].astype(x.dtype), window_strides=(1,),
                                 padding=[(SSM_CONV - 1, 0)],
                                 dimension_numbers=('NWC', 'WIO', 'NWC'),
                                 feature_group_count=c)
    return y + bias


def ssd_chunked_scan(xs, dt, a, bm, cm):
    b, s, g, j, p = xs.shape
    n = bm.shape[-1]
    L = SSM_CHUNK
    nc = s // L
    xdt = xs.astype(jnp.float32) * dt[..., None]
    da = dt * a

    def chunks(t):
        return jnp.moveaxis(t.reshape(b, nc, L, *t.shape[2:]), 1, 0)

    causal = jnp.tril(jnp.ones((L, L), dtype=bool))

    def step(state, inp):
        xc, dac, bc, cc = inp
        acum = jnp.cumsum(dac, axis=1)
        acum_t = jnp.moveaxis(acum, 1, -1)
        seg = acum_t[..., :, None] - acum_t[..., None, :]
        decay = jnp.exp(jnp.where(causal, seg, -jnp.inf))
        cb = jnp.einsum('blgn,bsgn->bgls', cc, bc)
        y_diag = jnp.einsum('bgls,bgjls,bsgjp->blgjp', cb, decay, xc)
        y_off = jnp.einsum('blgn,bgjpn,blgj->blgjp', cc, state, jnp.exp(acum))
        last = acum[:, -1]
        w_s = jnp.exp(last[:, None] - acum)
        new_state = state * jnp.exp(last)[..., None, None] + \
            jnp.einsum('blgj,blgjp,blgn->bgjpn', w_s, xc, bc)
        return new_state, y_diag + y_off

    state0 = jnp.zeros((b, g, j, p, n), jnp.float32)
    _, ys = lax.scan(step, state0, (chunks(xdt), chunks(da),
                                     chunks(bm.astype(jnp.float32)), chunks(cm.astype(jnp.float32))))
    return jnp.moveaxis(ys, 0, 1).reshape(b, s, g, j, p)


def hybrid_layer(x, norm_w, w_in, conv_w, conv_b, dt_bias, a_log, d_skip, ssm_norm_w,
                 w_attn_branch, w_ssm_branch, w_out):
    b, s, _ = x.shape
    hpg = SSM_HEADS // SSM_GROUPS
    hn = rms_norm(x, norm_w)
    proj = hn @ w_in
    split_points = []
    acc = 0
    for size in IN_SIZES[:-1]:
        acc += size
        split_points.append(acc)
    q, k, v, z_a, z_s, xbc, dt_raw, g_a, g_s = jnp.split(proj, split_points, axis=-1)

    q = q.reshape(b, s, ATTN_HEADS, ATTN_HEAD_DIM)
    k = k.reshape(b, s, ATTN_HEADS, ATTN_HEAD_DIM)
    v = v.reshape(b, s, ATTN_HEADS, ATTN_HEAD_DIM)
    slopes = alibi_slopes(ATTN_HEADS)
    outs = []
    lses = []
    for window, dilation in DILATED_PATTERNS:
        o, l = dilated_window_attention(q, k, v, window, dilation, slopes)
        outs.append(o)
        lses.append(l)
    wts = jax.nn.softmax(jnp.stack(lses), axis=0)
    o_a = jnp.einsum('pbsh,pbshe->bshe', wts.astype(q.dtype), jnp.stack(outs))
    o_a = o_a.reshape(b, s, ATTN_WIDTH) * jax.nn.silu(z_a)

    xbc = jax.nn.silu(causal_depthwise_conv(xbc, conv_w, conv_b))
    xs, bm, cm = jnp.split(xbc, [SSM_INNER, SSM_INNER + SSM_GROUPS * SSM_STATE], axis=-1)
    xs = xs.reshape(b, s, SSM_GROUPS, hpg, SSM_HEAD_DIM)
    bm = bm.reshape(b, s, SSM_GROUPS, SSM_STATE)
    cm = cm.reshape(b, s, SSM_GROUPS, SSM_STATE)
    dt = jax.nn.softplus(dt_raw.astype(jnp.float32) + dt_bias.astype(jnp.float32))
    dt = dt.reshape(b, s, SSM_GROUPS, hpg)
    a = -jnp.exp(a_log.astype(jnp.float32)).reshape(SSM_GROUPS, hpg)
    y = ssd_chunked_scan(xs, dt, a, bm, cm)
    y = y + d_skip.astype(jnp.float32).reshape(SSM_GROUPS, hpg)[..., None] * xs.astype(jnp.float32)
    y = y.reshape(b, s, SSM_INNER).astype(x.dtype)
    y = rms_norm(y * jax.nn.silu(z_s), ssm_norm_w)

    merged = jax.nn.sigmoid(g_a) * (o_a @ w_attn_branch) + jax.nn.sigmoid(g_s) * (y @ w_ssm_branch)
    return x + merged @ w_out


def setup_inputs(seed: int = 0) -> dict:
    key = jax.random.key(seed)
    ks = jax.random.split(key, 16)
    f32 = jnp.float32

    def dense(k, fan_in, fan_out):
        return jax.random.normal(k, (DEPTH, fan_in, fan_out), f32) * fan_in ** -0.5

    def gain(k, n):
        return 1.0 + 0.02 * jax.random.normal(k, (DEPTH, n), f32)

    x = jax.random.normal(ks[0], (BATCH, SEQ, D_MODEL), f32)
    norm_w = gain(ks[1], D_MODEL)
    w_in = dense(ks[2], D_MODEL, N_IN)
    conv_w = jax.random.normal(ks[3], (DEPTH, SSM_CONV, SSM_CONV_DIM), f32) * SSM_CONV ** -0.5
    conv_b = 0.01 * jax.random.normal(ks[4], (DEPTH, SSM_CONV_DIM), f32)
    u = jax.random.uniform(ks[5], (DEPTH, SSM_HEADS), f32)
    dt0 = jnp.exp(u * (math.log(0.1) - math.log(0.001)) + math.log(0.001))
    dt_bias = dt0 + jnp.log(-jnp.expm1(-dt0))
    a_log = jnp.log(jax.random.uniform(ks[6], (DEPTH, SSM_HEADS), f32, 1.0, 16.0))
    d_skip = gain(ks[7], SSM_HEADS)
    ssm_norm_w = gain(ks[8], SSM_INNER)
    w_attn_branch = dense(ks[9], ATTN_WIDTH, D_MODEL)
    w_ssm_branch = dense(ks[10], SSM_INNER, D_MODEL)
    w_out = dense(ks[11], D_MODEL, D_MODEL)
    final_norm_w = 1.0 + 0.02 * jax.random.normal(ks[12], (D_MODEL,), f32)
    return {'x': x, 'norm_w': norm_w, 'w_in': w_in, 'conv_w': conv_w, 'conv_b': conv_b,
            'dt_bias': dt_bias, 'a_log': a_log, 'd_skip': d_skip, 'ssm_norm_w': ssm_norm_w,
            'w_attn_branch': w_attn_branch, 'w_ssm_branch': w_ssm_branch, 'w_out': w_out,
            'final_norm_w': final_norm_w}


def reference(x, norm_w, w_in, conv_w, conv_b, dt_bias, a_log, d_skip, ssm_norm_w,
              w_attn_branch, w_ssm_branch, w_out, final_norm_w):
    for layer in range(DEPTH):
        x = hybrid_layer(x, norm_w[layer], w_in[layer], conv_w[layer], conv_b[layer],
                         dt_bias[layer], a_log[layer], d_skip[layer], ssm_norm_w[layer],
                         w_attn_branch[layer], w_ssm_branch[layer], w_out[layer])
    return rms_norm(x, final_norm_w)
```

```python
import functools
import math

import jax
import jax.numpy as jnp
from jax import lax
from jax.experimental import pallas as pl
from jax.experimental.pallas import tpu as pltpu

D_MODEL = 2048
ATTN_HEADS = 16
HEAD_DIM = 128
ATTN_WIDTH = ATTN_HEADS * HEAD_DIM
PATTERNS = ((128, 1), (512, 4), (2048, 16))
ATTN_BLOCK = 128

SSM_INNER = 4096
SSM_HEAD_DIM = 64
SSM_HEADS = 64
SSM_GROUPS = 8
SSM_STATE = 128
SSM_CONV = 4
SSM_CHUNK = 128
SSM_CONV_DIM = SSM_INNER + 2 * SSM_GROUPS * SSM_STATE
HEADS_PER_GROUP = SSM_HEADS // SSM_GROUPS
RMS_EPS = 1e-6

LANES = 128
CHUNK_W = D_MODEL
CH_Q, CH_K, CH_V, CH_ZA, CH_ZS, CH_XS, CH_BC, CH_GA, CH_GS = 0, 1, 2, 3, 4, 6, 8, 9, 10
N_CHUNKS = 11
DT_COL0 = 4 * ATTN_WIDTH + SSM_INNER + SSM_CONV_DIM

F32 = jnp.float32
BF16 = jnp.bfloat16
MIB = 1024 * 1024


def _sigmoid(v):
    return 1.0 / (1.0 + jnp.exp(-v))


def _compiler_params(n_grid, vmem_mib):
    return pltpu.CompilerParams(
        dimension_semantics=("arbitrary",) * n_grid,
        vmem_limit_bytes=vmem_mib * MIB,
    )


def _in_proj_kernel(x_ref, nw_ref, w_ref, wdt_ref, o_ref, dt_ref, hn_ref):
    @pl.when(pl.program_id(1) == 0)
    def _():
        x = x_ref[...]
        ms = jnp.mean(x * x, axis=-1, keepdims=True)
        hn = (x * lax.rsqrt(ms + RMS_EPS) * nw_ref[...]).astype(BF16)
        hn_ref[...] = hn
        dt_ref[...] = jnp.dot(hn, wdt_ref[...], preferred_element_type=F32)

    o_ref[...] = jnp.dot(hn_ref[...], w_ref[...], preferred_element_type=F32).astype(BF16)


def _in_proj(x2, norm_w, w_main, w_dt):
    t = x2.shape[0]
    tm = min(1024, t)
    tn = 1024
    per = CHUNK_W // tn
    return pl.pallas_call(
        _in_proj_kernel,
        grid=(t // tm, N_CHUNKS * per),
        in_specs=[
            pl.BlockSpec((tm, D_MODEL), lambda i, j: (i, 0)),
            pl.BlockSpec((1, D_MODEL), lambda i, j: (0, 0)),
            pl.BlockSpec((D_MODEL, tn), lambda i, j: (0, j)),
            pl.BlockSpec((D_MODEL, LANES), lambda i, j: (0, 0)),
        ],
        out_specs=[
            pl.BlockSpec((None, tm, tn), lambda i, j: (j // per, i, j % per)),
            pl.BlockSpec((tm, LANES), lambda i, j: (i, 0)),
        ],
        out_shape=[
            jax.ShapeDtypeStruct((N_CHUNKS, t, CHUNK_W), BF16),
            jax.ShapeDtypeStruct((t, LANES), F32),
        ],
        scratch_shapes=[pltpu.VMEM((tm, D_MODEL), BF16)],
        compiler_params=_compiler_params(2, 48),
        name="in_proj",
    )(x2, norm_w, w_main, w_dt)


def _attn_kernel(*refs, tq, has_prev, gate, slopes):
    it = iter(refs)
    q_ref, k_ref, v_ref, dist_ref, dist0_ref = (next(it) for _ in range(5))
    oprev_ref = lprev_ref = za_ref = l_ref = None
    if has_prev:
        oprev_ref, lprev_ref = next(it), next(it)
    if gate:
        za_ref = next(it)
    o_ref = next(it)
    if not gate:
        l_ref = next(it)
    kbuf, vbuf = next(it), next(it)

    i = pl.program_id(2)
    blk = ATTN_BLOCK

    @pl.when(i == 0)
    def _():
        kbuf[0:blk, :] = jnp.zeros((blk, ATTN_WIDTH), BF16)
        vbuf[0:blk, :] = jnp.zeros((blk, ATTN_WIDTH), BF16)

    @pl.when(i != 0)
    def _():
        kbuf[0:blk, :] = kbuf[tq:tq + blk, :]
        vbuf[0:blk, :] = vbuf[tq:tq + blk, :]

    kbuf[blk:blk + tq, :] = k_ref[...]
    vbuf[blk:blk + tq, :] = v_ref[...]

    scale = HEAD_DIM ** -0.5
    lane = lax.broadcasted_iota(jnp.int32, (blk, LANES), 1)

    def body(b, carry):
        r0 = pl.multiple_of(b * blk, blk)
        no_prev = jnp.logical_and(i == 0, b == 0)
        dist = jnp.where(no_prev, dist0_ref[...], dist_ref[...])
        lse_tile = jnp.zeros((blk, LANES), F32)
        lprev = lprev_ref[pl.ds(r0, blk), :] if has_prev else None
        for h in range(ATTN_HEADS):
            c0 = h * HEAD_DIM
            q = q_ref[pl.ds(r0, blk), c0:c0 + HEAD_DIM]
            kc = kbuf[pl.ds(r0, 2 * blk), c0:c0 + HEAD_DIM]
            vc = vbuf[pl.ds(r0, 2 * blk), c0:c0 + HEAD_DIM]
            s = lax.dot_general(q, kc, (((1,), (1,)), ((), ())), preferred_element_type=F32)
            s = s * scale - slopes[h] * dist
            m = jnp.max(s, axis=-1, keepdims=True)
            p = jnp.exp(s - m)
            l = jnp.sum(p, axis=-1, keepdims=True)
            acc = jnp.dot(p.astype(BF16), vc, preferred_element_type=F32)
            lse = m + jnp.log(l)
            if has_prev:
                lse_p = lprev[:, h:h + 1]
                top = jnp.maximum(lse, lse_p)
                w_own = jnp.exp(lse - top)
                w_prev = jnp.exp(lse_p - top)
                tot = w_own + w_prev
                o_prev = oprev_ref[pl.ds(r0, blk), c0:c0 + HEAD_DIM].astype(F32)
                o = acc * (w_own / (tot * l)) + o_prev * (w_prev / tot)
                lse = top + jnp.log(tot)
            else:
                o = acc * (1.0 / l)
            if gate:
                z = za_ref[pl.ds(r0, blk), c0:c0 + HEAD_DIM].astype(F32)
                o = o * (z * _sigmoid(z))
            else:
                lse_tile = jnp.where(lane == h, lse, lse_tile)
            o_ref[pl.ds(r0, blk), c0:c0 + HEAD_DIM] = o.astype(BF16)
        if not gate:
            l_ref[pl.ds(r0, blk), :] = lse_tile
        return carry

    lax.fori_loop(0, tq // blk, body, 0)


def _dist_tables(dilation):
    blk = ATTN_BLOCK
    qi = jnp.arange(blk)[:, None]
    ki = jnp.arange(2 * blk)[None, :]
    dist = qi - ki + blk
    valid = (dist >= 0) & (dist <= blk)
    d = jnp.where(valid, (dist * dilation).astype(F32), jnp.inf)
    d0 = jnp.where(ki >= blk, d, jnp.inf)
    return d, d0


def _attention_pattern(proj, dilation, batch, seq, prev, z_gate):
    sub = seq // dilation
    tq = min(512, sub)
    has_prev = prev is not None
    slopes = tuple(2.0 ** (-8.0 * (h + 1) / ATTN_HEADS) for h in range(ATTN_HEADS))
    pv = proj.reshape(N_CHUNKS, batch, sub, dilation * CHUNK_W)
    dist, dist0 = _dist_tables(dilation)

    def chunk_spec(ch):
        return pl.BlockSpec((None, None, tq, CHUNK_W), lambda b, r, i: (ch, b, i, r))

    row_spec = pl.BlockSpec((None, tq, CHUNK_W), lambda b, r, i: (b, i, r))
    lse_spec = pl.BlockSpec((None, tq, LANES), lambda b, r, i: (b, i, r))
    tab_spec = pl.BlockSpec((ATTN_BLOCK, 2 * ATTN_BLOCK), lambda b, r, i: (0, 0))

    in_specs = [chunk_spec(CH_Q), chunk_spec(CH_K), chunk_spec(CH_V), tab_spec, tab_spec]
    args = [pv, pv, pv, dist, dist0]
    if has_prev:
        o_prev, l_prev = prev
        in_specs += [row_spec, lse_spec]
        args += [o_prev.reshape(batch, sub, dilation * CHUNK_W),
                 l_prev.reshape(batch, sub, dilation * LANES)]
    if z_gate:
        in_specs += [chunk_spec(CH_ZA)]
        args += [pv]
    out_specs = [row_spec]
    out_shape = [jax.ShapeDtypeStruct((batch, sub, dilation * CHUNK_W), BF16)]
    if not z_gate:
        out_specs += [lse_spec]
        out_shape += [jax.ShapeDtypeStruct((batch, sub, dilation * LANES), F32)]

    outs = pl.pallas_call(
        functools.partial(_attn_kernel, tq=tq, has_prev=has_prev, gate=z_gate, slopes=slopes),
        grid=(batch, dilation, sub // tq),
        in_specs=in_specs,
        out_specs=out_specs,
        out_shape=out_shape,
        scratch_shapes=[pltpu.VMEM((tq + ATTN_BLOCK, ATTN_WIDTH), BF16),
                        pltpu.VMEM((tq + ATTN_BLOCK, ATTN_WIDTH), BF16)],
        compiler_params=_compiler_params(3, 48),
        name=f"attn_d{dilation}",
    )(*args)
    o = outs[0].reshape(batch, seq, CHUNK_W)
    if z_gate:
        return o
    return o, outs[1].reshape(batch, seq, LANES)


def _ssd_kernel(xs_ref, bc_ref, zs_ref, dt_ref, convw_ref, convb_ref, dtb_ref, alog_ref,
                dskip_ref, normw_ref, tril_ref, y_ref,
                xext, state, xs_s, b_s, c_s, acum_s, acumt_s, dt_s, xsc_s, y_s, *, ts):
    L = SSM_CHUNK
    n_state = SSM_STATE

    @pl.when(pl.program_id(1) == 0)
    def _():
        state[...] = jnp.zeros_like(state)
        xext[0:8, :] = jnp.zeros((8, SSM_CONV_DIM), F32)

    row = lax.broadcasted_iota(jnp.int32, (L, L), 0)
    col = lax.broadcasted_iota(jnp.int32, (L, L), 1)
    causal = row >= col
    low_half = col < SSM_HEAD_DIM
    neg_a = -jnp.exp(alog_ref[...])

    def chunk_body(c, carry):
        r0 = pl.multiple_of(c * L, L)

        xext[8:8 + L, 0:CHUNK_W] = xs_ref[0, pl.ds(r0, L), :].astype(F32)
        xext[8:8 + L, CHUNK_W:2 * CHUNK_W] = xs_ref[1, pl.ds(r0, L), :].astype(F32)
        xext[8:8 + L, 2 * CHUNK_W:3 * CHUNK_W] = bc_ref[pl.ds(r0, L), :].astype(F32)
        piece = 512
        for pc in range(SSM_CONV_DIM // piece):
            lo = pc * piece
            acc = jnp.broadcast_to(convb_ref[:, lo:lo + piece], (L, piece))
            for k in range(SSM_CONV):
                acc = acc + convw_ref[k:k + 1, lo:lo + piece] * xext[5 + k:5 + k + L, lo:lo + piece]
            xc = acc * _sigmoid(acc)
            if lo < SSM_INNER:
                xs_s[:, lo:lo + piece] = xc
            elif lo < SSM_INNER + SSM_GROUPS * n_state:
                b_s[:, lo - SSM_INNER:lo - SSM_INNER + piece] = xc.astype(BF16)
            else:
                o2 = lo - SSM_INNER - SSM_GROUPS * n_state
                c_s[:, o2:o2 + piece] = xc
        xext[0:8, :] = xext[L:L + 8, :]

        dt_in = dt_ref[pl.ds(r0, L), :] + dtb_ref[...]
        dtv = jnp.maximum(dt_in, 0.0) + jnp.log1p(jnp.exp(-jnp.abs(dt_in)))
        da = dtv * neg_a
        acum = jnp.dot(tril_ref[...], da, preferred_element_type=F32,
                       precision=lax.Precision.HIGHEST)
        acum_s[...] = acum
        acumt_s[...] = acum.T
        dt_s[...] = dtv

        for g in range(SSM_GROUPS):
            n0 = g * n_state
            bg = b_s[:, n0:n0 + n_state]
            cg = c_s[:, n0:n0 + n_state]
            cb = lax.dot_general(cg.astype(BF16), bg, (((1,), (1,)), ((), ())),
                                 preferred_element_type=F32)
            elast_tiles = []
            for jp in range(HEADS_PER_GROUP // 2):
                col0 = (g * (HEADS_PER_GROUP // 2) + jp) * LANES
                lhs = []
                cols = []
                for e in range(2):
                    h = g * HEADS_PER_GROUP + jp * 2 + e
                    col_a = acum_s[:, h:h + 1]
                    row_a = acumt_s[h:h + 1, :]
                    last = acum_s[L - 1:L, h:h + 1]
                    seg = col_a - row_a
                    decay = jnp.exp(jnp.where(causal, seg, -jnp.inf))
                    m_mat = (cb * decay).astype(BF16)
                    c_scaled = (cg * jnp.exp(col_a)).astype(BF16)
                    lhs.append(jnp.concatenate([m_mat, c_scaled], axis=1))
                    cols.append((dt_s[:, h:h + 1], jnp.exp(last - col_a), jnp.exp(last)))
                dt_pair = jnp.where(low_half, cols[0][0], cols[1][0])
                w_pair = jnp.where(low_half, cols[0][1], cols[1][1])
                elast_tiles.append(jnp.where(low_half[0:1, :], cols[0][2], cols[1][2]))
                xs_pair = xs_s[:, col0:col0 + LANES]
                xdt = xs_pair * dt_pair
                xsc_s[:, col0:col0 + LANES] = (xdt * w_pair).astype(BF16)
                rhs = jnp.concatenate([xdt.astype(BF16), state[:, col0:col0 + LANES].astype(BF16)],
                                      axis=0)
                y0 = jnp.dot(lhs[0], rhs, preferred_element_type=F32)
                y1 = jnp.dot(lhs[1], rhs, preferred_element_type=F32)
                y_s[:, col0:col0 + LANES] = (jnp.where(low_half, y0, y1)
                                             + dskip_ref[:, col0:col0 + LANES] * xs_pair)
            g0 = g * HEADS_PER_GROUP * SSM_HEAD_DIM
            gw = HEADS_PER_GROUP * SSM_HEAD_DIM
            upd = lax.dot_general(bg, xsc_s[:, g0:g0 + gw], (((0,), (0,)), ((), ())),
                                  preferred_element_type=F32)
            elast = jnp.concatenate(elast_tiles, axis=1)
            state[:, g0:g0 + gw] = state[:, g0:g0 + gw] * elast + upd

        ssq = jnp.zeros((L, 1), F32)
        for hf in range(2):
            z = zs_ref[hf, pl.ds(r0, L), :].astype(F32)
            gy = y_s[:, hf * CHUNK_W:(hf + 1) * CHUNK_W] * (z * _sigmoid(z))
            y_s[:, hf * CHUNK_W:(hf + 1) * CHUNK_W] = gy
            ssq = ssq + jnp.sum(gy * gy, axis=-1, keepdims=True)
        inv = lax.rsqrt(ssq * (1.0 / SSM_INNER) + RMS_EPS)
        y_ref[pl.ds(r0, L), :] = (y_s[...] * inv * normw_ref[...]).astype(BF16)
        return carry

    lax.fori_loop(0, ts // L, chunk_body, 0)


def _ssd(proj, dt_raw, conv_w, conv_b, dt_bias, a_log, d_skip, ssm_norm_w, batch, seq):
    t = batch * seq
    ts = min(512, seq)
    steps = seq // ts
    pad = LANES - SSM_HEADS
    dtb = jnp.pad(dt_bias.astype(F32), (0, pad)).reshape(1, LANES)
    alog = jnp.pad(a_log.astype(F32), (0, pad)).reshape(1, LANES)
    dskip = jnp.repeat(d_skip.astype(F32), SSM_HEAD_DIM).reshape(1, SSM_INNER)
    tril = jnp.tril(jnp.ones((SSM_CHUNK, SSM_CHUNK), F32))

    def full(shape):
        return pl.BlockSpec(shape, lambda b, i: (0,) * len(shape))

    return pl.pallas_call(
        functools.partial(_ssd_kernel, ts=ts),
        grid=(batch, steps),
        in_specs=[
            pl.BlockSpec((2, ts, CHUNK_W), lambda b, i: (CH_XS // 2, b * steps + i, 0)),
            pl.BlockSpec((None, ts, CHUNK_W), lambda b, i: (CH_BC, b * steps + i, 0)),
            pl.BlockSpec((2, ts, CHUNK_W), lambda b, i: (CH_ZS // 2, b * steps + i, 0)),
            pl.BlockSpec((ts, LANES), lambda b, i: (b * steps + i, 0)),
            full((SSM_CONV, SSM_CONV_DIM)),
            full((1, SSM_CONV_DIM)),
            full((1, LANES)),
            full((1, LANES)),
            full((1, SSM_INNER)),
            full((1, SSM_INNER)),
            full((SSM_CHUNK, SSM_CHUNK)),
        ],
        out_specs=pl.BlockSpec((ts, SSM_INNER), lambda b, i: (b * steps + i, 0)),
        out_shape=jax.ShapeDtypeStruct((t, SSM_INNER), BF16),
        scratch_shapes=[
            pltpu.VMEM((SSM_CHUNK + 8, SSM_CONV_DIM), F32),
            pltpu.VMEM((SSM_STATE, SSM_INNER), F32),
            pltpu.VMEM((SSM_CHUNK, SSM_INNER), F32),
            pltpu.VMEM((SSM_CHUNK, SSM_GROUPS * SSM_STATE), BF16),
            pltpu.VMEM((SSM_CHUNK, SSM_GROUPS * SSM_STATE), F32),
            pltpu.VMEM((SSM_CHUNK, LANES), F32),
            pltpu.VMEM((LANES, SSM_CHUNK), F32),
            pltpu.VMEM((SSM_CHUNK, LANES), F32),
            pltpu.VMEM((SSM_CHUNK, SSM_INNER), BF16),
            pltpu.VMEM((SSM_CHUNK, SSM_INNER), F32),
        ],
        compiler_params=_compiler_params(2, 48),
        name="ssd",
    )(proj, proj, proj, dt_raw, conv_w.astype(F32), conv_b.astype(F32).reshape(1, SSM_CONV_DIM),
      dtb, alog, dskip, ssm_norm_w.astype(F32).reshape(1, SSM_INNER), tril)


def _merge_kernel(oa_ref, yn_ref, ga_ref, gs_ref, wa_ref, ws_ref, o_ref):
    a = jnp.dot(oa_ref[...], wa_ref[...], preferred_element_type=F32)
    y = jnp.dot(yn_ref[...], ws_ref[...], preferred_element_type=F32)
    ga = _sigmoid(ga_ref[...].astype(F32))
    gs = _sigmoid(gs_ref[...].astype(F32))
    o_ref[...] = (ga * a + gs * y).astype(BF16)


def _merge(o_a, yn, proj, w_a, w_s):
    t = o_a.shape[0]
    tm = min(1024, t)
    tn = 512
    per = CHUNK_W // tn
    return pl.pallas_call(
        _merge_kernel,
        grid=(t // tm, per),
        in_specs=[
            pl.BlockSpec((tm, ATTN_WIDTH), lambda i, j: (i, 0)),
            pl.BlockSpec((tm, SSM_INNER), lambda i, j: (i, 0)),
            pl.BlockSpec((None, tm, tn), lambda i, j: (CH_GA, i, j)),
            pl.BlockSpec((None, tm, tn), lambda i, j: (CH_GS, i, j)),
            pl.BlockSpec((ATTN_WIDTH, tn), lambda i, j: (0, j)),
            pl.BlockSpec((SSM_INNER, tn), lambda i, j: (0, j)),
        ],
        out_specs=pl.BlockSpec((tm, tn), lambda i, j: (i, j)),
        out_shape=jax.ShapeDtypeStruct((t, D_MODEL), BF16),
        compiler_params=_compiler_params(2, 56),
        name="merge",
    )(o_a, yn, proj, proj, w_a, w_s)


def _out_kernel(m_ref, x_ref, w_ref, nw_ref, o_ref):
    r = x_ref[...] + jnp.dot(m_ref[...], w_ref[...], preferred_element_type=F32)
    ms = jnp.mean(r * r, axis=-1, keepdims=True)
    o_ref[...] = r * lax.rsqrt(ms + RMS_EPS) * nw_ref[...]


def _out_proj(merged, x2, w_out, final_norm_w):
    t = x2.shape[0]
    tm = min(512, t)
    return pl.pallas_call(
        _out_kernel,
        grid=(t // tm,),
        in_specs=[
            pl.BlockSpec((tm, D_MODEL), lambda i: (i, 0)),
            pl.BlockSpec((tm, D_MODEL), lambda i: (i, 0)),
            pl.BlockSpec((D_MODEL, D_MODEL), lambda i: (0, 0)),
            pl.BlockSpec((1, D_MODEL), lambda i: (0, 0)),
        ],
        out_specs=pl.BlockSpec((tm, D_MODEL), lambda i: (i, 0)),
        out_shape=jax.ShapeDtypeStruct((t, D_MODEL), F32),
        compiler_params=_compiler_params(1, 48),
        name="out_proj",
    )(merged, x2, w_out, final_norm_w)


def _layer(x, norm_w, w_in, conv_w, conv_b, dt_bias, a_log, d_skip, ssm_norm_w,
           w_attn_branch, w_ssm_branch, w_out, out_norm_w):
    batch, seq, _ = x.shape
    t = batch * seq
    x2 = x.reshape(t, D_MODEL)

    w_main = jnp.concatenate([w_in[:, :DT_COL0], w_in[:, DT_COL0 + SSM_HEADS:]], axis=1).astype(BF16)
    w_dt = jnp.pad(w_in[:, DT_COL0:DT_COL0 + SSM_HEADS], ((0, 0), (0, LANES - SSM_HEADS))).astype(BF16)
    proj, dt_raw = _in_proj(x2, norm_w.astype(F32).reshape(1, D_MODEL), w_main, w_dt)

    prev = None
    o_a = None
    for idx, (_, dilation) in enumerate(PATTERNS):
        last = idx == len(PATTERNS) - 1
        res = _attention_pattern(proj, dilation, batch, seq, prev, last)
        if last:
            o_a = res
        else:
            prev = res

    yn = _ssd(proj, dt_raw, conv_w, conv_b, dt_bias, a_log, d_skip, ssm_norm_w, batch, seq)

    merged = _merge(o_a.reshape(t, ATTN_WIDTH), yn, proj,
                    w_attn_branch.astype(BF16), w_ssm_branch.astype(BF16))
    out = _out_proj(merged, x2, w_out.astype(BF16), out_norm_w.astype(F32).reshape(1, D_MODEL))
    return out.reshape(batch, seq, D_MODEL)


def kernel(x, norm_w, w_in, conv_w, conv_b, dt_bias, a_log, d_skip, ssm_norm_w,
           w_attn_branch, w_ssm_branch, w_out, final_norm_w):
    depth = norm_w.shape[0]
    assert depth == 1, "the residual stream between layers is not normalised; only depth 1 is fused"
    assert x.shape[1] % (PATTERNS[-1][1] * ATTN_BLOCK) == 0 and x.shape[2] == D_MODEL
    return _layer(x, norm_w[0], w_in[0], conv_w[0], conv_b[0], dt_bias[0], a_log[0], d_skip[0],
                  ssm_norm_w[0], w_attn_branch[0], w_ssm_branch[0], w_out[0], final_norm_w)
```

```python
import functools

import jax
import jax.numpy as jnp
from jax import lax
from jax.experimental import pallas as pl
from jax.experimental.pallas import tpu as pltpu

D_MODEL = 2048
ATTN_HEADS = 16
HEAD_DIM = 128
ATTN_WIDTH = ATTN_HEADS * HEAD_DIM
PATTERNS = ((128, 1), (512, 4), (2048, 16))
ATTN_BLOCK = 128

SSM_INNER = 4096
SSM_HEAD_DIM = 64
SSM_HEADS = 64
SSM_GROUPS = 8
SSM_STATE = 128
SSM_CONV = 4
SSM_CHUNK = 128
SSM_CONV_DIM = SSM_INNER + 2 * SSM_GROUPS * SSM_STATE
HEADS_PER_GROUP = SSM_HEADS // SSM_GROUPS
RMS_EPS = 1e-6

LANES = 128
CHUNK_W = D_MODEL
CH_Q, CH_K, CH_V, CH_ZA, CH_ZS, CH_XS, CH_BC, CH_GA, CH_GS = 0, 1, 2, 3, 4, 6, 8, 9, 10
N_CHUNKS = 11
DT_COL0 = 4 * ATTN_WIDTH + SSM_INNER + SSM_CONV_DIM

F32 = jnp.float32
BF16 = jnp.bfloat16
MIB = 1024 * 1024


def _sigmoid(v):
    return 1.0 / (1.0 + jnp.exp(-v))


def _compiler_params(n_grid, vmem_mib):
    return pltpu.CompilerParams(
        dimension_semantics=("arbitrary",) * n_grid,
        vmem_limit_bytes=vmem_mib * MIB,
    )


def _in_proj_kernel(x_ref, nw_ref, w_ref, wdt_ref, o_ref, dt_ref, hn_ref):
    @pl.when(pl.program_id(1) == 0)
    def _():
        x = x_ref[...]
        ms = jnp.mean(x * x, axis=-1, keepdims=True)
        hn = (x * lax.rsqrt(ms + RMS_EPS) * nw_ref[...]).astype(BF16)
        hn_ref[...] = hn
        dt_ref[...] = jnp.dot(hn, wdt_ref[...], preferred_element_type=F32)

    o_ref[...] = jnp.dot(hn_ref[...], w_ref[...], preferred_element_type=F32).astype(BF16)


def _in_proj(x2, norm_w, w_main, w_dt):
    t = x2.shape[0]
    tm = min(1024, t)
    tn = 1024
    per = CHUNK_W // tn
    return pl.pallas_call(
        _in_proj_kernel,
        grid=(t // tm, N_CHUNKS * per),
        in_specs=[
            pl.BlockSpec((tm, D_MODEL), lambda i, j: (i, 0)),
            pl.BlockSpec((1, D_MODEL), lambda i, j: (0, 0)),
            pl.BlockSpec((D_MODEL, tn), lambda i, j: (0, j)),
            pl.BlockSpec((D_MODEL, LANES), lambda i, j: (0, 0)),
        ],
        out_specs=[
            pl.BlockSpec((None, tm, tn), lambda i, j: (j // per, i, j % per)),
            pl.BlockSpec((tm, LANES), lambda i, j: (i, 0)),
        ],
        out_shape=[
            jax.ShapeDtypeStruct((N_CHUNKS, t, CHUNK_W), BF16),
            jax.ShapeDtypeStruct((t, LANES), F32),
        ],
        scratch_shapes=[pltpu.VMEM((tm, D_MODEL), BF16)],
        compiler_params=_compiler_params(2, 48),
        name="in_proj",
    )(x2, norm_w, w_main, w_dt)


ATTN_ROWS = PATTERNS[-1][1] * ATTN_BLOCK
ATTN_HEADS_PER_STEP = 2
ATTN_STEP_W = ATTN_HEADS_PER_STEP * HEAD_DIM
ATTN_UNROLL = 4


def _attn_kernel(q_ref, k_ref, v_ref, z_ref, dist_ref, dist0_ref, slope_ref, o_ref,
                 slab, kd, vd, qd, oacc, lacc):
    i = pl.program_id(2)
    blk = ATTN_BLOCK
    rows = ATTN_ROWS
    nh = ATTN_HEADS_PER_STEP
    scale = HEAD_DIM ** -0.5

    geo = []
    kv_base = q_base = 0
    for _, dil in PATTERNS:
        sub = rows // dil
        geo.append((dil, sub, kv_base, sub + blk, q_base))
        kv_base += dil * (sub + blk)
        q_base += rows if dil > 1 else 0

    prev_rows = [(base + r * pitch, sub) for dil, sub, base, pitch, _ in geo for r in range(dil)]

    @pl.when(i == 0)
    def _():
        for p0, _ in prev_rows:
            kd[p0:p0 + blk, :] = jnp.zeros((blk, ATTN_STEP_W), BF16)
            vd[p0:p0 + blk, :] = jnp.zeros((blk, ATTN_STEP_W), BF16)

    @pl.when(i != 0)
    def _():
        for p0, sub in prev_rows:
            kd[p0:p0 + blk, :] = kd[p0 + sub:p0 + sub + blk, :]
            vd[p0:p0 + blk, :] = vd[p0 + sub:p0 + sub + blk, :]

    def regroup(src_ref, dst, is_kv):
        for s in range(nh):
            slab[s] = src_ref[:, s * LANES:(s + 1) * LANES].astype(F32)
        for dil, sub, base, pitch, qb in geo:
            if dil == 1 and not is_kv:
                continue
            for r in range(dil):
                p0 = base + r * pitch + blk if is_kv else qb + r * sub
                for s in range(nh):
                    if dil == 1:
                        val = src_ref[:, s * LANES:(s + 1) * LANES]
                    else:
                        val = slab[s, pl.ds(r, sub, stride=dil), :].astype(BF16)
                    dst[p0:p0 + sub, s * LANES:(s + 1) * LANES] = val

    regroup(k_ref, kd, True)
    regroup(v_ref, vd, True)
    regroup(q_ref, qd, False)

    ones = jnp.ones((2 * blk, LANES), BF16)

    def unit(q, kc, vc, dist, slope):
        s = lax.dot_general(q, kc, (((1,), (1,)), ((), ())), preferred_element_type=F32)
        s = s * scale - slope * dist
        m = jnp.max(s, axis=-1, keepdims=True)
        p = jnp.exp(s - m)
        pv = jnp.dot(p.astype(BF16), jnp.concatenate([vc, ones], axis=1),
                     preferred_element_type=F32)
        l_b = pv[:, LANES:]
        m_b = jnp.broadcast_to(m, (blk, LANES))
        return pv[:, :LANES] * (1.0 / l_b), m_b + jnp.log(l_b)

    for pi, (dil, sub, base, pitch, qb) in enumerate(geo):
        nblk = sub // blk

        def body(it, carry, dil=dil, sub=sub, base=base, pitch=pitch, qb=qb, nblk=nblk, pi=pi):
            r = it // nblk
            b = it % nblk
            no_prev = jnp.logical_and(i == 0, b == 0)
            dist = jnp.where(no_prev, dist0_ref[...], dist_ref[...])
            q0 = pl.multiple_of(qb + r * sub + b * blk, blk)
            k0 = pl.multiple_of(base + r * pitch + b * blk, blk)
            t0 = r + b * blk * dil
            if dil == 1:
                tok = pl.ds(pl.multiple_of(t0, blk), blk)
            else:
                tok = pl.ds(t0, blk, stride=dil)
            for s in range(nh):
                c0 = s * LANES
                q_src = q_ref if dil == 1 else qd
                o, lse = unit(q_src[pl.ds(q0, blk), c0:c0 + LANES],
                              kd[pl.ds(k0, 2 * blk), c0:c0 + LANES],
                              vd[pl.ds(k0, 2 * blk), c0:c0 + LANES], dist, slope_ref[s] * float(dil))
                oacc[pi * nh + s, tok, :] = o
                lacc[pi * nh + s, tok, :] = lse
            return carry

        lax.fori_loop(0, dil * nblk, body, 0, unroll=ATTN_UNROLL)

    n_pat = len(geo)

    def mix(bi, carry):
        tok = pl.ds(pl.multiple_of(bi * blk, blk), blk)
        for s in range(nh):
            lses = [lacc[pi * nh + s, tok, :] for pi in range(n_pat)]
            top = functools.reduce(jnp.maximum, lses)
            ws = [jnp.exp(l - top) for l in lses]
            num = sum(w * oacc[pi * nh + s, tok, :] for pi, w in enumerate(ws))
            z = z_ref[tok, s * LANES:(s + 1) * LANES].astype(F32)
            o_ref[tok, s * LANES:(s + 1) * LANES] = (
                num * (1.0 / sum(ws)) * (z * _sigmoid(z))).astype(BF16)
        return carry

    lax.fori_loop(0, rows // blk, mix, 0, unroll=2)


def _dist_tables():
    blk = ATTN_BLOCK
    qi = jnp.arange(blk)[:, None]
    ki = jnp.arange(2 * blk)[None, :]
    dist = qi - ki + blk
    valid = (dist >= 0) & (dist <= blk)
    d = jnp.where(valid, dist.astype(F32), jnp.inf)
    d0 = jnp.where(ki >= blk, d, jnp.inf)
    return d, d0


def _attention(proj, batch, seq):
    assert all(w // d == ATTN_BLOCK for w, d in PATTERNS)
    t = batch * seq
    rows = ATTN_ROWS
    steps = seq // rows
    n_groups = ATTN_HEADS // ATTN_HEADS_PER_STEP
    slopes = jnp.asarray([2.0 ** (-8.0 * (h + 1) / ATTN_HEADS) for h in range(ATTN_HEADS)], F32)
    slope_tab = jnp.broadcast_to(slopes[:, None, None], (ATTN_HEADS, 1, 2 * ATTN_BLOCK))
    dist, dist0 = _dist_tables()
    kv_rows = sum(d * (rows // d + ATTN_BLOCK) for _, d in PATTERNS)
    q_rows = sum(rows for _, d in PATTERNS if d > 1)

    def chunk_spec(ch):
        return pl.BlockSpec((None, rows, ATTN_STEP_W), lambda b, g, i: (ch, b * steps + i, g))

    tab_spec = pl.BlockSpec((ATTN_BLOCK, 2 * ATTN_BLOCK), lambda b, g, i: (0, 0))
    slope_spec = pl.BlockSpec((ATTN_HEADS_PER_STEP, 1, 2 * ATTN_BLOCK), lambda b, g, i: (g, 0, 0))
    return pl.pallas_call(
        _attn_kernel,
        grid=(batch, n_groups, steps),
        in_specs=[chunk_spec(CH_Q), chunk_spec(CH_K), chunk_spec(CH_V), chunk_spec(CH_ZA),
                  tab_spec, tab_spec, slope_spec],
        out_specs=pl.BlockSpec((rows, ATTN_STEP_W), lambda b, g, i: (b * steps + i, g)),
        out_shape=jax.ShapeDtypeStruct((t, ATTN_WIDTH), BF16),
        scratch_shapes=[
            pltpu.VMEM((ATTN_HEADS_PER_STEP, rows, LANES), F32),
            pltpu.VMEM((kv_rows, ATTN_STEP_W), BF16),
            pltpu.VMEM((kv_rows, ATTN_STEP_W), BF16),
            pltpu.VMEM((q_rows, ATTN_STEP_W), BF16),
            pltpu.VMEM((len(PATTERNS) * ATTN_HEADS_PER_STEP, rows, LANES), F32),
            pltpu.VMEM((len(PATTERNS) * ATTN_HEADS_PER_STEP, rows, LANES), F32),
        ],
        compiler_params=_compiler_params(3, 48),
        name="attn",
    )(proj, proj, proj, proj, dist, dist0, slope_tab)


def _ssd_kernel(xs_ref, bc_ref, zs_ref, dt_ref, convw_ref, convb_ref, dtb_ref, alog_ref,
                dskip_ref, normw_ref, tril_ref, y_ref,
                xext, state, xs_s, b_s, c_s, acum_s, acumt_s, dt_s, xsc_s, y_s, *, ts):
    L = SSM_CHUNK
    n_state = SSM_STATE

    @pl.when(pl.program_id(1) == 0)
    def _():
        state[...] = jnp.zeros_like(state)
        xext[0:8, :] = jnp.zeros((8, SSM_CONV_DIM), F32)

    row = lax.broadcasted_iota(jnp.int32, (L, L), 0)
    col = lax.broadcasted_iota(jnp.int32, (L, L), 1)
    causal = row >= col
    low_half = col < SSM_HEAD_DIM
    neg_a = -jnp.exp(alog_ref[...])

    def chunk_body(c, carry):
        r0 = pl.multiple_of(c * L, L)

        xext[8:8 + L, 0:CHUNK_W] = xs_ref[0, pl.ds(r0, L), :].astype(F32)
        xext[8:8 + L, CHUNK_W:2 * CHUNK_W] = xs_ref[1, pl.ds(r0, L), :].astype(F32)
        xext[8:8 + L, 2 * CHUNK_W:3 * CHUNK_W] = bc_ref[pl.ds(r0, L), :].astype(F32)
        piece = 512
        for pc in range(SSM_CONV_DIM // piece):
            lo = pc * piece
            acc = jnp.broadcast_to(convb_ref[:, lo:lo + piece], (L, piece))
            for k in range(SSM_CONV):
                acc = acc + convw_ref[k:k + 1, lo:lo + piece] * xext[5 + k:5 + k + L, lo:lo + piece]
            xc = acc * _sigmoid(acc)
            if lo < SSM_INNER:
                xs_s[:, lo:lo + piece] = xc
            elif lo < SSM_INNER + SSM_GROUPS * n_state:
                b_s[:, lo - SSM_INNER:lo - SSM_INNER + piece] = xc.astype(BF16)
            else:
                o2 = lo - SSM_INNER - SSM_GROUPS * n_state
                c_s[:, o2:o2 + piece] = xc
        xext[0:8, :] = xext[L:L + 8, :]

        dt_in = dt_ref[pl.ds(r0, L), :] + dtb_ref[...]
        dtv = jnp.maximum(dt_in, 0.0) + jnp.log(1.0 + jnp.exp(-jnp.abs(dt_in)))
        da = dtv * neg_a
        acum = jnp.dot(tril_ref[...], da, preferred_element_type=F32,
                       precision=lax.Precision.HIGHEST)
        acum_s[...] = acum
        acumt_s[...] = acum.T
        dt_s[...] = dtv

        for g in range(SSM_GROUPS):
            n0 = g * n_state
            bg = b_s[:, n0:n0 + n_state]
            cg = c_s[:, n0:n0 + n_state]
            cb = lax.dot_general(cg.astype(BF16), bg, (((1,), (1,)), ((), ())),
                                 preferred_element_type=F32)
            elast_tiles = []
            for jp in range(HEADS_PER_GROUP // 2):
                col0 = (g * (HEADS_PER_GROUP // 2) + jp) * LANES
                lhs = []
                cols = []
                for e in range(2):
                    h = g * HEADS_PER_GROUP + jp * 2 + e
                    col_a = acum_s[:, h:h + 1]
                    row_a = acumt_s[h:h + 1, :]
                    last = acum_s[L - 1:L, h:h + 1]
                    seg = col_a - row_a
                    decay = jnp.exp(jnp.where(causal, seg, -jnp.inf))
                    m_mat = (cb * decay).astype(BF16)
                    c_scaled = (cg * jnp.exp(col_a)).astype(BF16)
                    lhs.append(jnp.concatenate([m_mat, c_scaled], axis=1))
                    cols.append((dt_s[:, h:h + 1], jnp.exp(last - col_a), jnp.exp(last)))
                dt_pair = jnp.where(low_half, cols[0][0], cols[1][0])
                w_pair = jnp.where(low_half, cols[0][1], cols[1][1])
                elast_tiles.append(jnp.where(low_half[0:1, :], cols[0][2], cols[1][2]))
                xs_pair = xs_s[:, col0:col0 + LANES]
                xdt = xs_pair * dt_pair
                xsc_s[:, col0:col0 + LANES] = (xdt * w_pair).astype(BF16)
                rhs = jnp.concatenate([xdt.astype(BF16), state[:, col0:col0 + LANES].astype(BF16)],
                                      axis=0)
                y0 = jnp.dot(lhs[0], rhs, preferred_element_type=F32)
                y1 = jnp.dot(lhs[1], rhs, preferred_element_type=F32)
                y_s[:, col0:col0 + LANES] = (jnp.where(low_half, y0, y1)
                                             + dskip_ref[:, col0:col0 + LANES] * xs_pair)
            g0 = g * HEADS_PER_GROUP * SSM_HEAD_DIM
            gw = HEADS_PER_GROUP * SSM_HEAD_DIM
            upd = lax.dot_general(bg, xsc_s[:, g0:g0 + gw], (((0,), (0,)), ((), ())),
                                  preferred_element_type=F32)
            elast = jnp.concatenate(elast_tiles, axis=1)
            state[:, g0:g0 + gw] = state[:, g0:g0 + gw] * elast + upd

        ssq = jnp.zeros((L, 1), F32)
        for hf in range(2):
            z = zs_ref[hf, pl.ds(r0, L), :].astype(F32)
            gy = y_s[:, hf * CHUNK_W:(hf + 1) * CHUNK_W] * (z * _sigmoid(z))
            y_s[:, hf * CHUNK_W:(hf + 1) * CHUNK_W] = gy
            ssq = ssq + jnp.sum(gy * gy, axis=-1, keepdims=True)
        inv = lax.rsqrt(ssq * (1.0 / SSM_INNER) + RMS_EPS)
        y_ref[pl.ds(r0, L), :] = (y_s[...] * inv * normw_ref[...]).astype(BF16)
        return carry

    lax.fori_loop(0, ts // L, chunk_body, 0)


def _ssd(proj, dt_raw, conv_w, conv_b, dt_bias, a_log, d_skip, ssm_norm_w, batch, seq):
    t = batch * seq
    ts = min(512, seq)
    steps = seq // ts
    pad = LANES - SSM_HEADS
    dtb = jnp.pad(dt_bias.astype(F32), (0, pad)).reshape(1, LANES)
    alog = jnp.pad(a_log.astype(F32), (0, pad)).reshape(1, LANES)
    dskip = jnp.repeat(d_skip.astype(F32), SSM_HEAD_DIM).reshape(1, SSM_INNER)
    tril = jnp.tril(jnp.ones((SSM_CHUNK, SSM_CHUNK), F32))

    def full(shape):
        return pl.BlockSpec(shape, lambda b, i: (0,) * len(shape))

    return pl.pallas_call(
        functools.partial(_ssd_kernel, ts=ts),
        grid=(batch, steps),
        in_specs=[
            pl.BlockSpec((2, ts, CHUNK_W), lambda b, i: (CH_XS // 2, b * steps + i, 0)),
            pl.BlockSpec((None, ts, CHUNK_W), lambda b, i: (CH_BC, b * steps + i, 0)),
            pl.BlockSpec((2, ts, CHUNK_W), lambda b, i: (CH_ZS // 2, b * steps + i, 0)),
            pl.BlockSpec((ts, LANES), lambda b, i: (b * steps + i, 0)),
            full((SSM_CONV, SSM_CONV_DIM)),
            full((1, SSM_CONV_DIM)),
            full((1, LANES)),
            full((1, LANES)),
            full((1, SSM_INNER)),
            full((1, SSM_INNER)),
            full((SSM_CHUNK, SSM_CHUNK)),
        ],
        out_specs=pl.BlockSpec((ts, SSM_INNER), lambda b, i: (b * steps + i, 0)),
        out_shape=jax.ShapeDtypeStruct((t, SSM_INNER), BF16),
        scratch_shapes=[
            pltpu.VMEM((SSM_CHUNK + 8, SSM_CONV_DIM), F32),
            pltpu.VMEM((SSM_STATE, SSM_INNER), F32),
            pltpu.VMEM((SSM_CHUNK, SSM_INNER), F32),
            pltpu.VMEM((SSM_CHUNK, SSM_GROUPS * SSM_STATE), BF16),
            pltpu.VMEM((SSM_CHUNK, SSM_GROUPS * SSM_STATE), F32),
            pltpu.VMEM((SSM_CHUNK, LANES), F32),
            pltpu.VMEM((LANES, SSM_CHUNK), F32),
            pltpu.VMEM((SSM_CHUNK, LANES), F32),
            pltpu.VMEM((SSM_CHUNK, SSM_INNER), BF16),
            pltpu.VMEM((SSM_CHUNK, SSM_INNER), F32),
        ],
        compiler_params=_compiler_params(2, 48),
        name="ssd",
    )(proj, proj, proj, dt_raw, conv_w.astype(F32), conv_b.astype(F32).reshape(1, SSM_CONV_DIM),
      dtb, alog, dskip, ssm_norm_w.astype(F32).reshape(1, SSM_INNER), tril)


def _merge_kernel(oa_ref, yn_ref, ga_ref, gs_ref, wa_ref, ws_ref, o_ref):
    a = jnp.dot(oa_ref[...], wa_ref[...], preferred_element_type=F32)
    y = jnp.dot(yn_ref[...], ws_ref[...], preferred_element_type=F32)
    ga = _sigmoid(ga_ref[...].astype(F32))
    gs = _sigmoid(gs_ref[...].astype(F32))
    o_ref[...] = (ga * a + gs * y).astype(BF16)


def _merge(o_a, yn, proj, w_a, w_s):
    t = o_a.shape[0]
    tm = min(1024, t)
    tn = 512
    per = CHUNK_W // tn
    return pl.pallas_call(
        _merge_kernel,
        grid=(t // tm, per),
        in_specs=[
            pl.BlockSpec((tm, ATTN_WIDTH), lambda i, j: (i, 0)),
            pl.BlockSpec((tm, SSM_INNER), lambda i, j: (i, 0)),
            pl.BlockSpec((None, tm, tn), lambda i, j: (CH_GA, i, j)),
            pl.BlockSpec((None, tm, tn), lambda i, j: (CH_GS, i, j)),
            pl.BlockSpec((ATTN_WIDTH, tn), lambda i, j: (0, j)),
            pl.BlockSpec((SSM_INNER, tn), lambda i, j: (0, j)),
        ],
        out_specs=pl.BlockSpec((tm, tn), lambda i, j: (i, j)),
        out_shape=jax.ShapeDtypeStruct((t, D_MODEL), BF16),
        compiler_params=_compiler_params(2, 56),
        name="merge",
    )(o_a, yn, proj, proj, w_a, w_s)


def _out_kernel(m_ref, x_ref, w_ref, nw_ref, o_ref):
    r = x_ref[...] + jnp.dot(m_ref[...], w_ref[...], preferred_element_type=F32)
    ms = jnp.mean(r * r, axis=-1, keepdims=True)
    o_ref[...] = r * lax.rsqrt(ms + RMS_EPS) * nw_ref[...]


def _out_proj(merged, x2, w_out, final_norm_w):
    t = x2.shape[0]
    tm = min(512, t)
    return pl.pallas_call(
        _out_kernel,
        grid=(t // tm,),
        in_specs=[
            pl.BlockSpec((tm, D_MODEL), lambda i: (i, 0)),
            pl.BlockSpec((tm, D_MODEL), lambda i: (i, 0)),
            pl.BlockSpec((D_MODEL, D_MODEL), lambda i: (0, 0)),
            pl.BlockSpec((1, D_MODEL), lambda i: (0, 0)),
        ],
        out_specs=pl.BlockSpec((tm, D_MODEL), lambda i: (i, 0)),
        out_shape=jax.ShapeDtypeStruct((t, D_MODEL), F32),
        compiler_params=_compiler_params(1, 48),
        name="out_proj",
    )(merged, x2, w_out, final_norm_w)


def _layer(x, norm_w, w_in, conv_w, conv_b, dt_bias, a_log, d_skip, ssm_norm_w,
           w_attn_branch, w_ssm_branch, w_out, out_norm_w):
    batch, seq, _ = x.shape
    t = batch * seq
    x2 = x.reshape(t, D_MODEL)

    w_main = jnp.concatenate([w_in[:, :DT_COL0], w_in[:, DT_COL0 + SSM_HEADS:]], axis=1).astype(BF16)
    w_dt = jnp.pad(w_in[:, DT_COL0:DT_COL0 + SSM_HEADS], ((0, 0), (0, LANES - SSM_HEADS))).astype(BF16)
    proj, dt_raw = _in_proj(x2, norm_w.astype(F32).reshape(1, D_MODEL), w_main, w_dt)

    o_a = _attention(proj, batch, seq)
    yn = _ssd(proj, dt_raw, conv_w, conv_b, dt_bias, a_log, d_skip, ssm_norm_w, batch, seq)

    merged = _merge(o_a, yn, proj, w_attn_branch.astype(BF16), w_ssm_branch.astype(BF16))
    out = _out_proj(merged, x2, w_out.astype(BF16), out_norm_w.astype(F32).reshape(1, D_MODEL))
    return out.reshape(batch, seq, D_MODEL)


def kernel(x, norm_w, w_in, conv_w, conv_b, dt_bias, a_log, d_skip, ssm_norm_w,
           w_attn_branch, w_ssm_branch, w_out, final_norm_w):
    depth = norm_w.shape[0]
    assert depth == 1, "the residual stream between layers is not normalised; only depth 1 is fused"
    assert x.shape[1] % ATTN_ROWS == 0 and x.shape[2] == D_MODEL
    return _layer(x, norm_w[0], w_in[0], conv_w[0], conv_b[0], dt_bias[0], a_log[0], d_skip[0],
                  ssm_norm_w[0], w_attn_branch[0], w_ssm_branch[0], w_out[0], final_norm_w)
```

```python
import functools

import jax
import jax.numpy as jnp
from jax import lax
from jax.experimental import pallas as pl
from jax.experimental.pallas import tpu as pltpu

D_MODEL = 2048
ATTN_HEADS = 16
HEAD_DIM = 128
ATTN_WIDTH = ATTN_HEADS * HEAD_DIM
PATTERNS = ((128, 1), (512, 4), (2048, 16))
ATTN_BLOCK = 128

SSM_INNER = 4096
SSM_HEAD_DIM = 64
SSM_HEADS = 64
SSM_GROUPS = 8
SSM_STATE = 128
SSM_CONV = 4
SSM_CHUNK = 128
SSM_CONV_DIM = SSM_INNER + 2 * SSM_GROUPS * SSM_STATE
HEADS_PER_GROUP = SSM_HEADS // SSM_GROUPS
RMS_EPS = 1e-6

LANES = 128
CHUNK_W = D_MODEL
CH_Q, CH_K, CH_V, CH_ZA, CH_ZS, CH_XS, CH_BC, CH_GA, CH_GS = 0, 1, 2, 3, 4, 6, 8, 9, 10
N_CHUNKS = 11
DT_COL0 = 4 * ATTN_WIDTH + SSM_INNER + SSM_CONV_DIM

F32 = jnp.float32
BF16 = jnp.bfloat16
MIB = 1024 * 1024


def _sigmoid(v):
    return 1.0 / (1.0 + jnp.exp(-v))


def _compiler_params(n_grid, vmem_mib):
    return pltpu.CompilerParams(
        dimension_semantics=("arbitrary",) * n_grid,
        vmem_limit_bytes=vmem_mib * MIB,
    )


def _in_proj_kernel(x_ref, nw_ref, w_ref, wdt_ref, o_ref, dt_ref, hn_ref):
    @pl.when(pl.program_id(1) == 0)
    def _():
        x = x_ref[...]
        ms = jnp.mean(x * x, axis=-1, keepdims=True)
        hn = (x * lax.rsqrt(ms + RMS_EPS) * nw_ref[...]).astype(BF16)
        hn_ref[...] = hn
        dt_ref[...] = jnp.dot(hn, wdt_ref[...], preferred_element_type=F32)

    o_ref[...] = jnp.dot(hn_ref[...], w_ref[...], preferred_element_type=F32).astype(BF16)


def _in_proj(x2, norm_w, w_main, w_dt):
    t = x2.shape[0]
    tm = min(1024, t)
    tn = 1024
    per = CHUNK_W // tn
    return pl.pallas_call(
        _in_proj_kernel,
        grid=(t // tm, N_CHUNKS * per),
        in_specs=[
            pl.BlockSpec((tm, D_MODEL), lambda i, j: (i, 0)),
            pl.BlockSpec((1, D_MODEL), lambda i, j: (0, 0)),
            pl.BlockSpec((D_MODEL, tn), lambda i, j: (0, j)),
            pl.BlockSpec((D_MODEL, LANES), lambda i, j: (0, 0)),
        ],
        out_specs=[
            pl.BlockSpec((None, tm, tn), lambda i, j: (j // per, i, j % per)),
            pl.BlockSpec((tm, LANES), lambda i, j: (i, 0)),
        ],
        out_shape=[
            jax.ShapeDtypeStruct((N_CHUNKS, t, CHUNK_W), BF16),
            jax.ShapeDtypeStruct((t, LANES), F32),
        ],
        scratch_shapes=[pltpu.VMEM((tm, D_MODEL), BF16)],
        compiler_params=_compiler_params(2, 48),
        name="in_proj",
    )(x2, norm_w, w_main, w_dt)


ATTN_ROWS = PATTERNS[-1][1] * ATTN_BLOCK
ATTN_HEADS_PER_STEP = 2
ATTN_STEP_W = ATTN_HEADS_PER_STEP * HEAD_DIM
ATTN_UNROLL = 8


def _attn_kernel(q_ref, k_ref, v_ref, z_ref, dist_ref, dist0_ref, slope_ref, o_ref,
                 slab, kd, vd, qd, oacc, lacc):
    i = pl.program_id(2)
    blk = ATTN_BLOCK
    rows = ATTN_ROWS
    nh = ATTN_HEADS_PER_STEP
    scale = HEAD_DIM ** -0.5

    geo = []
    kv_base = q_base = 0
    for _, dil in PATTERNS:
        sub = rows // dil
        geo.append((dil, sub, kv_base, sub + blk, q_base))
        kv_base += dil * (sub + blk)
        q_base += rows if dil > 1 else 0

    prev_rows = [(base + r * pitch, sub) for dil, sub, base, pitch, _ in geo for r in range(dil)]

    @pl.when(i == 0)
    def _():
        for p0, _ in prev_rows:
            kd[p0:p0 + blk, :] = jnp.zeros((blk, ATTN_STEP_W), BF16)
            vd[p0:p0 + blk, :] = jnp.zeros((blk, ATTN_STEP_W), BF16)

    @pl.when(i != 0)
    def _():
        for p0, sub in prev_rows:
            kd[p0:p0 + blk, :] = kd[p0 + sub:p0 + sub + blk, :]
            vd[p0:p0 + blk, :] = vd[p0 + sub:p0 + sub + blk, :]

    def regroup(src_ref, dst, is_kv):
        for s in range(nh):
            slab[s] = src_ref[:, s * LANES:(s + 1) * LANES].astype(F32)
        for dil, sub, base, pitch, qb in geo:
            if dil == 1 and not is_kv:
                continue
            for r in range(dil):
                p0 = base + r * pitch + blk if is_kv else qb + r * sub
                for s in range(nh):
                    if dil == 1:
                        val = src_ref[:, s * LANES:(s + 1) * LANES]
                    else:
                        val = slab[s, pl.ds(r, sub, stride=dil), :].astype(BF16)
                    dst[p0:p0 + sub, s * LANES:(s + 1) * LANES] = val

    regroup(k_ref, kd, True)
    regroup(v_ref, vd, True)
    regroup(q_ref, qd, False)

    ones = jnp.ones((2 * blk, LANES), BF16)

    def unit(q, kc, vc, dist, slope):
        s = lax.dot_general(q, kc, (((1,), (1,)), ((), ())), preferred_element_type=F32)
        s = s * scale - slope * dist
        m = jnp.max(s, axis=-1, keepdims=True)
        p = jnp.exp(s - m)
        pv = jnp.dot(p.astype(BF16), jnp.concatenate([vc, ones], axis=1),
                     preferred_element_type=F32)
        l_b = pv[:, LANES:]
        m_b = jnp.broadcast_to(m, (blk, LANES))
        return pv[:, :LANES] * (1.0 / l_b), m_b + jnp.log(l_b)

    for pi, (dil, sub, base, pitch, qb) in enumerate(geo):
        nblk = sub // blk

        def body(it, carry, dil=dil, sub=sub, base=base, pitch=pitch, qb=qb, nblk=nblk, pi=pi):
            r = it // nblk
            b = it % nblk
            no_prev = jnp.logical_and(i == 0, b == 0)
            dist = jnp.where(no_prev, dist0_ref[...], dist_ref[...])
            q0 = pl.multiple_of(qb + r * sub + b * blk, blk)
            k0 = pl.multiple_of(base + r * pitch + b * blk, blk)
            t0 = r + b * blk * dil
            if dil == 1:
                tok = pl.ds(pl.multiple_of(t0, blk), blk)
            else:
                tok = pl.ds(t0, blk, stride=dil)
            for s in range(nh):
                c0 = s * LANES
                q_src = q_ref if dil == 1 else qd
                o, lse = unit(q_src[pl.ds(q0, blk), c0:c0 + LANES],
                              kd[pl.ds(k0, 2 * blk), c0:c0 + LANES],
                              vd[pl.ds(k0, 2 * blk), c0:c0 + LANES], dist, slope_ref[s] * float(dil))
                oacc[pi * nh + s, tok, :] = o
                lacc[pi * nh + s, tok, :] = lse
            return carry

        lax.fori_loop(0, dil * nblk, body, 0, unroll=ATTN_UNROLL)

    n_pat = len(geo)

    def mix(bi, carry):
        tok = pl.ds(pl.multiple_of(bi * blk, blk), blk)
        for s in range(nh):
            lses = [lacc[pi * nh + s, tok, :] for pi in range(n_pat)]
            top = functools.reduce(jnp.maximum, lses)
            ws = [jnp.exp(l - top) for l in lses]
            num = sum(w * oacc[pi * nh + s, tok, :] for pi, w in enumerate(ws))
            z = z_ref[tok, s * LANES:(s + 1) * LANES].astype(F32)
            o_ref[tok, s * LANES:(s + 1) * LANES] = (
                num * (1.0 / sum(ws)) * (z * _sigmoid(z))).astype(BF16)
        return carry

    lax.fori_loop(0, rows // blk, mix, 0, unroll=2)


def _dist_tables():
    blk = ATTN_BLOCK
    qi = jnp.arange(blk)[:, None]
    ki = jnp.arange(2 * blk)[None, :]
    dist = qi - ki + blk
    valid = (dist >= 0) & (dist <= blk)
    d = jnp.where(valid, dist.astype(F32), jnp.inf)
    d0 = jnp.where(ki >= blk, d, jnp.inf)
    return d, d0


def _attention(proj, batch, seq):
    assert all(w // d == ATTN_BLOCK for w, d in PATTERNS)
    t = batch * seq
    rows = ATTN_ROWS
    steps = seq // rows
    n_groups = ATTN_HEADS // ATTN_HEADS_PER_STEP
    slopes = jnp.asarray([2.0 ** (-8.0 * (h + 1) / ATTN_HEADS) for h in range(ATTN_HEADS)], F32)
    slope_tab = jnp.broadcast_to(slopes[:, None, None], (ATTN_HEADS, 1, 2 * ATTN_BLOCK))
    dist, dist0 = _dist_tables()
    kv_rows = sum(d * (rows // d + ATTN_BLOCK) for _, d in PATTERNS)
    q_rows = sum(rows for _, d in PATTERNS if d > 1)

    def chunk_spec(ch):
        return pl.BlockSpec((None, rows, ATTN_STEP_W), lambda b, g, i: (ch, b * steps + i, g))

    tab_spec = pl.BlockSpec((ATTN_BLOCK, 2 * ATTN_BLOCK), lambda b, g, i: (0, 0))
    slope_spec = pl.BlockSpec((ATTN_HEADS_PER_STEP, 1, 2 * ATTN_BLOCK), lambda b, g, i: (g, 0, 0))
    return pl.pallas_call(
        _attn_kernel,
        grid=(batch, n_groups, steps),
        in_specs=[chunk_spec(CH_Q), chunk_spec(CH_K), chunk_spec(CH_V), chunk_spec(CH_ZA),
                  tab_spec, tab_spec, slope_spec],
        out_specs=pl.BlockSpec((rows, ATTN_STEP_W), lambda b, g, i: (b * steps + i, g)),
        out_shape=jax.ShapeDtypeStruct((t, ATTN_WIDTH), BF16),
        scratch_shapes=[
            pltpu.VMEM((ATTN_HEADS_PER_STEP, rows, LANES), F32),
            pltpu.VMEM((kv_rows, ATTN_STEP_W), BF16),
            pltpu.VMEM((kv_rows, ATTN_STEP_W), BF16),
            pltpu.VMEM((q_rows, ATTN_STEP_W), BF16),
            pltpu.VMEM((len(PATTERNS) * ATTN_HEADS_PER_STEP, rows, LANES), F32),
            pltpu.VMEM((len(PATTERNS) * ATTN_HEADS_PER_STEP, rows, LANES), F32),
        ],
        compiler_params=_compiler_params(3, 48),
        name="attn",
    )(proj, proj, proj, proj, dist, dist0, slope_tab)


def _ssd_kernel(xs_ref, bc_ref, zs_ref, dt_ref, convw_ref, convb_ref, dtb_ref, alog_ref,
                dskip_ref, normw_ref, tril_ref, ws_ref, y_ref,
                xext, state, xs_s, b_s, c_s, acum_s, acumt_s, dt_s, xsc_s, y_s, ynbuf,
                *, ts, steps):
    L = SSM_CHUNK
    n_state = SSM_STATE
    n_slabs = SSM_CONV_DIM // LANES
    g_step = pl.program_id(0)
    cur = g_step % 2

    @pl.when(g_step == 0)
    def _():
        ynbuf[1] = jnp.zeros((ts, SSM_INNER), BF16)

    @pl.when(g_step % steps == 0)
    def _():
        state[...] = jnp.zeros_like(state)
        for s in range(n_slabs):
            xext[s, 0:8, :] = jnp.zeros((8, LANES), F32)

    row = lax.broadcasted_iota(jnp.int32, (L, L), 0)
    col = lax.broadcasted_iota(jnp.int32, (L, L), 1)
    causal = row >= col
    low_half = col < SSM_HEAD_DIM
    neg_a = -jnp.exp(alog_ref[...])

    def chunk_body(c, carry):
        r0 = pl.multiple_of(c * L, L)

        per_chunk = CHUNK_W // LANES
        for s in range(n_slabs):
            lo = s * LANES
            c0 = (s % per_chunk) * LANES
            if s < 2 * per_chunk:
                raw = xs_ref[s // per_chunk, pl.ds(r0, L), c0:c0 + LANES]
            else:
                raw = bc_ref[pl.ds(r0, L), c0:c0 + LANES]
            xext[s, 8:8 + L, :] = raw.astype(F32)
            acc = jnp.broadcast_to(convb_ref[:, lo:lo + LANES], (L, LANES))
            for k in range(SSM_CONV):
                shift = 8 - (SSM_CONV - 1) + k
                acc = acc + convw_ref[k:k + 1, lo:lo + LANES] * xext[s, shift:shift + L, :]
            xext[s, 0:8, :] = xext[s, L:L + 8, :]
            xc = acc * _sigmoid(acc)
            if lo < SSM_INNER:
                xs_s[:, lo:lo + LANES] = xc
            elif lo < SSM_INNER + SSM_GROUPS * n_state:
                b_s[:, lo - SSM_INNER:lo - SSM_INNER + LANES] = xc.astype(BF16)
            else:
                o2 = lo - SSM_INNER - SSM_GROUPS * n_state
                c_s[:, o2:o2 + LANES] = xc

        dt_in = dt_ref[pl.ds(r0, L), :] + dtb_ref[...]
        dtv = jnp.maximum(dt_in, 0.0) + jnp.log(1.0 + jnp.exp(-jnp.abs(dt_in)))
        da = dtv * neg_a
        acum = jnp.dot(tril_ref[...], da, preferred_element_type=F32,
                       precision=lax.Precision.HIGHEST)
        acum_s[...] = acum
        acumt_s[...] = acum.T
        dt_s[...] = dtv

        proj_w = D_MODEL // SSM_GROUPS
        for g in range(SSM_GROUPS):
            p0 = g * proj_w
            y_ref[pl.ds(r0, L), p0:p0 + proj_w] = jnp.dot(
                ynbuf[1 - cur, pl.ds(r0, L), :], ws_ref[:, p0:p0 + proj_w],
                preferred_element_type=F32).astype(BF16)

            n0 = g * n_state
            bg = b_s[:, n0:n0 + n_state]
            cg = c_s[:, n0:n0 + n_state]
            cb = lax.dot_general(cg.astype(BF16), bg, (((1,), (1,)), ((), ())),
                                 preferred_element_type=F32)
            elast_tiles = []
            for jp in range(HEADS_PER_GROUP // 2):
                col0 = (g * (HEADS_PER_GROUP // 2) + jp) * LANES
                lhs = []
                cols = []
                for e in range(2):
                    h = g * HEADS_PER_GROUP + jp * 2 + e
                    col_a = acum_s[:, h:h + 1]
                    row_a = acumt_s[h:h + 1, :]
                    last = acum_s[L - 1:L, h:h + 1]
                    seg = col_a - row_a
                    decay = jnp.exp(jnp.where(causal, seg, -jnp.inf))
                    m_mat = (cb * decay).astype(BF16)
                    c_scaled = (cg * jnp.exp(col_a)).astype(BF16)
                    lhs.append(jnp.concatenate([m_mat, c_scaled], axis=1))
                    cols.append((dt_s[:, h:h + 1], jnp.exp(last - col_a), jnp.exp(last)))
                dt_pair = jnp.where(low_half, cols[0][0], cols[1][0])
                w_pair = jnp.where(low_half, cols[0][1], cols[1][1])
                elast_tiles.append(jnp.where(low_half[0:1, :], cols[0][2], cols[1][2]))
                xs_pair = xs_s[:, col0:col0 + LANES]
                xdt = xs_pair * dt_pair
                xsc_s[:, col0:col0 + LANES] = (xdt * w_pair).astype(BF16)
                rhs = jnp.concatenate([xdt.astype(BF16), state[:, col0:col0 + LANES].astype(BF16)],
                                      axis=0)
                y0 = jnp.dot(lhs[0], rhs, preferred_element_type=F32)
                y1 = jnp.dot(lhs[1], rhs, preferred_element_type=F32)
                y_s[:, col0:col0 + LANES] = (jnp.where(low_half, y0, y1)
                                             + dskip_ref[:, col0:col0 + LANES] * xs_pair)
            g0 = g * HEADS_PER_GROUP * SSM_HEAD_DIM
            gw = HEADS_PER_GROUP * SSM_HEAD_DIM
            upd = lax.dot_general(bg, xsc_s[:, g0:g0 + gw], (((0,), (0,)), ((), ())),
                                  preferred_element_type=F32)
            elast = jnp.concatenate(elast_tiles, axis=1)
            state[:, g0:g0 + gw] = state[:, g0:g0 + gw] * elast + upd

        ssq = jnp.zeros((L, 1), F32)
        for hf in range(2):
            z = zs_ref[hf, pl.ds(r0, L), :].astype(F32)
            gy = y_s[:, hf * CHUNK_W:(hf + 1) * CHUNK_W] * (z * _sigmoid(z))
            y_s[:, hf * CHUNK_W:(hf + 1) * CHUNK_W] = gy
            ssq = ssq + jnp.sum(gy * gy, axis=-1, keepdims=True)
        inv = lax.rsqrt(ssq * (1.0 / SSM_INNER) + RMS_EPS)
        ynbuf[cur, pl.ds(r0, L), :] = (y_s[...] * inv * normw_ref[...]).astype(BF16)
        return carry

    lax.fori_loop(0, ts // L, chunk_body, 0)


def _ssd(proj, dt_raw, conv_w, conv_b, dt_bias, a_log, d_skip, ssm_norm_w, w_s, batch, seq):
    t = batch * seq
    ts = min(256, seq)
    steps = seq // ts
    n_steps = batch * steps
    last = n_steps - 1
    pad = LANES - SSM_HEADS
    dtb = jnp.pad(dt_bias.astype(F32), (0, pad)).reshape(1, LANES)
    alog = jnp.pad(a_log.astype(F32), (0, pad)).reshape(1, LANES)
    dskip = jnp.repeat(d_skip.astype(F32), SSM_HEAD_DIM).reshape(1, SSM_INNER)
    tril = jnp.tril(jnp.ones((SSM_CHUNK, SSM_CHUNK), F32))

    def full(shape, **kw):
        return pl.BlockSpec(shape, lambda g: (0,) * len(shape), **kw)

    def rows_of(g):
        return jnp.minimum(g, last)

    return pl.pallas_call(
        functools.partial(_ssd_kernel, ts=ts, steps=steps),
        grid=(n_steps + 1,),
        in_specs=[
            pl.BlockSpec((2, ts, CHUNK_W), lambda g: (CH_XS // 2, rows_of(g), 0)),
            pl.BlockSpec((None, ts, CHUNK_W), lambda g: (CH_BC, rows_of(g), 0)),
            pl.BlockSpec((2, ts, CHUNK_W), lambda g: (CH_ZS // 2, rows_of(g), 0)),
            pl.BlockSpec((ts, LANES), lambda g: (rows_of(g), 0)),
            full((SSM_CONV, SSM_CONV_DIM)),
            full((1, SSM_CONV_DIM)),
            full((1, LANES)),
            full((1, LANES)),
            full((1, SSM_INNER)),
            full((1, SSM_INNER)),
            full((SSM_CHUNK, SSM_CHUNK)),
            full((SSM_INNER, D_MODEL), pipeline_mode=pl.Buffered(1)),
        ],
        out_specs=pl.BlockSpec((ts, D_MODEL), lambda g: (jnp.maximum(g - 1, 0), 0)),
        out_shape=jax.ShapeDtypeStruct((t, D_MODEL), BF16),
        scratch_shapes=[
            pltpu.VMEM((SSM_CONV_DIM // LANES, SSM_CHUNK + 8, LANES), F32),
            pltpu.VMEM((SSM_STATE, SSM_INNER), F32),
            pltpu.VMEM((SSM_CHUNK, SSM_INNER), F32),
            pltpu.VMEM((SSM_CHUNK, SSM_GROUPS * SSM_STATE), BF16),
            pltpu.VMEM((SSM_CHUNK, SSM_GROUPS * SSM_STATE), F32),
            pltpu.VMEM((SSM_CHUNK, LANES), F32),
            pltpu.VMEM((LANES, SSM_CHUNK), F32),
            pltpu.VMEM((SSM_CHUNK, LANES), F32),
            pltpu.VMEM((SSM_CHUNK, SSM_INNER), BF16),
            pltpu.VMEM((SSM_CHUNK, SSM_INNER), F32),
            pltpu.VMEM((2, ts, SSM_INNER), BF16),
        ],
        compiler_params=_compiler_params(1, 56),
        name="ssd",
    )(proj, proj, proj, dt_raw, conv_w.astype(F32), conv_b.astype(F32).reshape(1, SSM_CONV_DIM),
      dtb, alog, dskip, ssm_norm_w.astype(F32).reshape(1, SSM_INNER), tril, w_s)


def _tail_kernel(oa_ref, ys_ref, ga_ref, gs_ref, x_ref, wa_ref, wo_ref, nw_ref, o_ref):
    a = jnp.dot(oa_ref[...], wa_ref[...], preferred_element_type=F32)
    ga = _sigmoid(ga_ref[...].astype(F32))
    gs = _sigmoid(gs_ref[...].astype(F32))
    merged = (ga * a + gs * ys_ref[...].astype(F32)).astype(BF16)
    r = x_ref[...] + jnp.dot(merged, wo_ref[...], preferred_element_type=F32)
    ms = jnp.mean(r * r, axis=-1, keepdims=True)
    o_ref[...] = r * lax.rsqrt(ms + RMS_EPS) * nw_ref[...]


def _tail(o_a, y_s, proj, x2, w_a, w_out, final_norm_w):
    t = x2.shape[0]
    tm = min(256, t)

    def rows(shape):
        return pl.BlockSpec(shape, lambda i: (i, 0))

    def resident(shape):
        return pl.BlockSpec(shape, lambda i: (0, 0), pipeline_mode=pl.Buffered(1))

    return pl.pallas_call(
        _tail_kernel,
        grid=(t // tm,),
        in_specs=[
            rows((tm, ATTN_WIDTH)),
            rows((tm, D_MODEL)),
            pl.BlockSpec((None, tm, CHUNK_W), lambda i: (CH_GA, i, 0)),
            pl.BlockSpec((None, tm, CHUNK_W), lambda i: (CH_GS, i, 0)),
            rows((tm, D_MODEL)),
            resident((ATTN_WIDTH, D_MODEL)),
            resident((D_MODEL, D_MODEL)),
            resident((1, D_MODEL)),
        ],
        out_specs=rows((tm, D_MODEL)),
        out_shape=jax.ShapeDtypeStruct((t, D_MODEL), F32),
        compiler_params=_compiler_params(1, 48),
        name="tail",
    )(o_a, y_s, proj, proj, x2, w_a, w_out, final_norm_w)


def _layer(x, norm_w, w_in, conv_w, conv_b, dt_bias, a_log, d_skip, ssm_norm_w,
           w_attn_branch, w_ssm_branch, w_out, out_norm_w):
    batch, seq, _ = x.shape
    t = batch * seq
    x2 = x.reshape(t, D_MODEL)

    w_main = jnp.concatenate([w_in[:, :DT_COL0], w_in[:, DT_COL0 + SSM_HEADS:]], axis=1).astype(BF16)
    w_dt = jnp.pad(w_in[:, DT_COL0:DT_COL0 + SSM_HEADS], ((0, 0), (0, LANES - SSM_HEADS))).astype(BF16)
    proj, dt_raw = _in_proj(x2, norm_w.astype(F32).reshape(1, D_MODEL), w_main, w_dt)

    o_a = _attention(proj, batch, seq)
    y_s = _ssd(proj, dt_raw, conv_w, conv_b, dt_bias, a_log, d_skip, ssm_norm_w,
               w_ssm_branch.astype(BF16), batch, seq)
    out = _tail(o_a, y_s, proj, x2, w_attn_branch.astype(BF16), w_out.astype(BF16),
                out_norm_w.astype(F32).reshape(1, D_MODEL))
    return out.reshape(batch, seq, D_MODEL)


def kernel(x, norm_w, w_in, conv_w, conv_b, dt_bias, a_log, d_skip, ssm_norm_w,
           w_attn_branch, w_ssm_branch, w_out, final_norm_w):
    depth = norm_w.shape[0]
    assert depth == 1, "the residual stream between layers is not normalised; only depth 1 is fused"
    assert x.shape[1] % ATTN_ROWS == 0 and x.shape[2] == D_MODEL
    return _layer(x, norm_w[0], w_in[0], conv_w[0], conv_b[0], dt_bias[0], a_log[0], d_skip[0],
                  ssm_norm_w[0], w_attn_branch[0], w_ssm_branch[0], w_out[0], final_norm_w)
```

```python
import functools

import jax
import jax.numpy as jnp
from jax import lax
from jax.experimental import pallas as pl
from jax.experimental.pallas import tpu as pltpu

D_MODEL = 2048
ATTN_HEADS = 16
HEAD_DIM = 128
ATTN_WIDTH = ATTN_HEADS * HEAD_DIM
PATTERNS = ((128, 1), (512, 4), (2048, 16))
ATTN_BLOCK = 128

SSM_INNER = 4096
SSM_HEAD_DIM = 64
SSM_HEADS = 64
SSM_GROUPS = 8
SSM_STATE = 128
SSM_CONV = 4
SSM_CHUNK = 128
SSM_CONV_DIM = SSM_INNER + 2 * SSM_GROUPS * SSM_STATE
HEADS_PER_GROUP = SSM_HEADS // SSM_GROUPS
RMS_EPS = 1e-6
LOG2E = 1.4426950408889634

LANES = 128
CHUNK_W = D_MODEL
CH_Q, CH_K, CH_V, CH_ZA, CH_ZS, CH_XS, CH_BC, CH_GA, CH_GS = 0, 1, 2, 3, 4, 6, 8, 9, 10
N_CHUNKS = 11
DT_COL0 = 4 * ATTN_WIDTH + SSM_INNER + SSM_CONV_DIM

F32 = jnp.float32
BF16 = jnp.bfloat16
MIB = 1024 * 1024


def _sigmoid(v):
    return 1.0 / (1.0 + jnp.exp(-v))


def _compiler_params(n_grid, vmem_mib):
    return pltpu.CompilerParams(
        dimension_semantics=("arbitrary",) * n_grid,
        vmem_limit_bytes=vmem_mib * MIB,
    )


def _in_proj_kernel(x_ref, nw_ref, w_ref, wg_ref, wdt_ref, o_ref, dt_ref, hn_ref, *, n_lead):
    j = pl.program_id(1)

    @pl.when(j == 0)
    def _():
        x = x_ref[...]
        ms = jnp.mean(x * x, axis=-1, keepdims=True)
        hn = (x * lax.rsqrt(ms + RMS_EPS) * nw_ref[...]).astype(BF16)
        hn_ref[...] = hn
        dt_ref[...] = jnp.dot(hn, wdt_ref[...], preferred_element_type=F32)

    @pl.when(j < n_lead)
    def _():
        o_ref[...] = jnp.dot(hn_ref[...], w_ref[...], preferred_element_type=F32).astype(BF16)

    @pl.when(j >= n_lead)
    def _():
        o_ref[...] = jnp.dot(hn_ref[...], wg_ref[...], preferred_element_type=F32).astype(BF16)


def _in_proj(x2, norm_w, w_all, w_gates, w_dt):
    t = x2.shape[0]
    tm = min(1024, t)
    tn = 1024
    per = CHUNK_W // tn
    n_lead = DT_COL0 // tn
    return pl.pallas_call(
        functools.partial(_in_proj_kernel, n_lead=n_lead),
        grid=(t // tm, N_CHUNKS * per),
        in_specs=[
            pl.BlockSpec((tm, D_MODEL), lambda i, j: (i, 0)),
            pl.BlockSpec((1, D_MODEL), lambda i, j: (0, 0)),
            pl.BlockSpec((D_MODEL, tn), lambda i, j: (0, jnp.minimum(j, n_lead - 1))),
            pl.BlockSpec((D_MODEL, tn), lambda i, j: (0, jnp.maximum(j - n_lead, 0))),
            pl.BlockSpec((D_MODEL, LANES), lambda i, j: (0, 0)),
        ],
        out_specs=[
            pl.BlockSpec((None, tm, tn), lambda i, j: (j // per, i, j % per)),
            pl.BlockSpec((tm, LANES), lambda i, j: (i, 0)),
        ],
        out_shape=[
            jax.ShapeDtypeStruct((N_CHUNKS, t, CHUNK_W), BF16),
            jax.ShapeDtypeStruct((t, LANES), F32),
        ],
        scratch_shapes=[pltpu.VMEM((tm, D_MODEL), BF16)],
        compiler_params=_compiler_params(2, 48),
        name="in_proj",
    )(x2, norm_w, w_all, w_gates, w_dt)


ATTN_ROWS = PATTERNS[-1][1] * ATTN_BLOCK
ATTN_HEADS_PER_STEP = 2
ATTN_STEP_W = ATTN_HEADS_PER_STEP * HEAD_DIM
ATTN_UNROLL = 8


def _attn_kernel(q_ref, k_ref, v_ref, z_ref, dist_ref, dist0_ref, slope_ref, o_ref,
                 slabs, kd, vd, qd, oacc, lacc, bias):
    i = pl.program_id(2)
    blk = ATTN_BLOCK
    rows = ATTN_ROWS
    nh = ATTN_HEADS_PER_STEP
    scale = HEAD_DIM ** -0.5

    geo = []
    kv_base = q_base = 0
    for _, dil in PATTERNS:
        sub = rows // dil
        geo.append((dil, sub, kv_base, sub + blk, q_base))
        kv_base += dil * (sub + blk)
        q_base += rows if dil > 1 else 0

    prev_rows = [(base + r * pitch, sub) for dil, sub, base, pitch, _ in geo for r in range(dil)]

    @pl.when(i == 0)
    def _():
        for p0, _ in prev_rows:
            kd[p0:p0 + blk, :] = jnp.zeros((blk, ATTN_STEP_W), BF16)
            vd[p0:p0 + blk, :] = jnp.zeros((blk, ATTN_STEP_W), BF16)

    @pl.when(i != 0)
    def _():
        for p0, sub in prev_rows:
            kd[p0:p0 + blk, :] = kd[p0 + sub:p0 + sub + blk, :]
            vd[p0:p0 + blk, :] = vd[p0 + sub:p0 + sub + blk, :]

    def regroup(src_ref, dst, is_kv):
        def stage(rb, carry):
            rws = pl.ds(pl.multiple_of(rb * blk, blk), blk)
            for s in range(nh):
                slabs[0, s, rws, :] = src_ref[rws, s * LANES:(s + 1) * LANES].astype(F32)
            return carry

        lax.fori_loop(0, rows // blk, stage, 0, unroll=4)
        level, d_prev, sub_prev = 0, 1, rows
        for li, (dil, sub, base, pitch, qb) in enumerate(geo):
            nblk = sub // blk
            if dil == 1:
                if is_kv:
                    def copy(rb, carry, base=base):
                        r0 = pl.multiple_of(rb * blk, blk)
                        dst[pl.ds(base + blk + r0, blk), :] = src_ref[pl.ds(r0, blk), :]
                        return carry

                    lax.fori_loop(0, nblk, copy, 0, unroll=4)
                continue
            ratio = dil // d_prev
            keep = li < len(geo) - 1

            def gather(it, carry, sub=sub, base=base, pitch=pitch, qb=qb, nblk=nblk,
                       ratio=ratio, d_prev=d_prev, sub_prev=sub_prev, level=level, keep=keep):
                r = it // nblk
                b = it % nblk
                src0 = (r % d_prev) * sub_prev + r // d_prev + b * blk * ratio
                p0 = base + r * pitch + blk if is_kv else qb + r * sub
                p0 = pl.multiple_of(p0 + b * blk, blk)
                for s in range(nh):
                    val = slabs[level, s, pl.ds(src0, blk, stride=ratio), :]
                    dst[pl.ds(p0, blk), s * LANES:(s + 1) * LANES] = val.astype(BF16)
                    if keep:
                        slabs[1 - level, s, pl.ds(pl.multiple_of(r * sub + b * blk, blk), blk), :] = val
                return carry

            lax.fori_loop(0, dil * nblk, gather, 0, unroll=4)
            level, d_prev, sub_prev = 1 - level, dil, sub

    regroup(k_ref, kd, True)
    regroup(v_ref, vd, True)
    regroup(q_ref, qd, False)

    for pi, (dil, _, _, _, _) in enumerate(geo):
        for s in range(nh):
            slope = slope_ref[s] * (float(dil) * LOG2E)
            bias[(pi * nh + s) * 2] = slope * dist_ref[...]
            bias[(pi * nh + s) * 2 + 1] = slope * dist0_ref[...]

    ones = jnp.ones((2 * blk, LANES), BF16)

    def unit(q, kc, vc, bias_t):
        s = lax.dot_general(q, kc, (((1,), (1,)), ((), ())), preferred_element_type=F32)
        s = s * (scale * LOG2E) - bias_t
        m = jnp.max(s, axis=-1, keepdims=True)
        p = jnp.exp2(s - m)
        pv = jnp.dot(p.astype(BF16), jnp.concatenate([vc, ones], axis=1),
                     preferred_element_type=F32)
        l_b = pv[:, LANES:]
        m_b = jnp.broadcast_to(m, (blk, LANES))
        return pv[:, :LANES] * (1.0 / l_b), m_b + jnp.log(l_b) * LOG2E

    for pi, (dil, sub, base, pitch, qb) in enumerate(geo):
        nblk = sub // blk

        def body(it, carry, dil=dil, sub=sub, base=base, pitch=pitch, qb=qb, nblk=nblk, pi=pi):
            r = it // nblk
            b = it % nblk
            no_prev = jnp.logical_and(i == 0, b == 0).astype(jnp.int32)
            q0 = pl.multiple_of(qb + r * sub + b * blk, blk)
            k0 = pl.multiple_of(base + r * pitch + b * blk, blk)
            t0 = r + b * blk * dil
            if dil == 1:
                tok = pl.ds(pl.multiple_of(t0, blk), blk)
            else:
                tok = pl.ds(t0, blk, stride=dil)
            for s in range(nh):
                c0 = s * LANES
                q_src = q_ref if dil == 1 else qd
                o, lse = unit(q_src[pl.ds(q0, blk), c0:c0 + LANES],
                              kd[pl.ds(k0, 2 * blk), c0:c0 + LANES],
                              vd[pl.ds(k0, 2 * blk), c0:c0 + LANES],
                              bias[(pi * nh + s) * 2 + no_prev])
                oacc[pi * nh + s, tok, :] = o
                lacc[pi * nh + s, tok, :] = lse
            return carry

        lax.fori_loop(0, dil * nblk, body, 0, unroll=ATTN_UNROLL)

    n_pat = len(geo)

    def mix(bi, carry):
        tok = pl.ds(pl.multiple_of(bi * blk, blk), blk)
        for s in range(nh):
            lses = [lacc[pi * nh + s, tok, :] for pi in range(n_pat)]
            top = functools.reduce(jnp.maximum, lses)
            ws = [jnp.exp2(l - top) for l in lses]
            num = sum(w * oacc[pi * nh + s, tok, :] for pi, w in enumerate(ws))
            z = z_ref[tok, s * LANES:(s + 1) * LANES].astype(F32)
            o_ref[tok, s * LANES:(s + 1) * LANES] = (
                num * (1.0 / sum(ws)) * (z * _sigmoid(z))).astype(BF16)
        return carry

    lax.fori_loop(0, rows // blk, mix, 0, unroll=2)


def _dist_tables():
    blk = ATTN_BLOCK
    qi = jnp.arange(blk)[:, None]
    ki = jnp.arange(2 * blk)[None, :]
    dist = qi - ki + blk
    valid = (dist >= 0) & (dist <= blk)
    d = jnp.where(valid, dist.astype(F32), jnp.inf)
    d0 = jnp.where(ki >= blk, d, jnp.inf)
    return d, d0


def _attention(proj, batch, seq):
    assert all(w // d == ATTN_BLOCK for w, d in PATTERNS)
    t = batch * seq
    rows = ATTN_ROWS
    steps = seq // rows
    n_groups = ATTN_HEADS // ATTN_HEADS_PER_STEP
    slopes = jnp.asarray([2.0 ** (-8.0 * (h + 1) / ATTN_HEADS) for h in range(ATTN_HEADS)], F32)
    slope_tab = jnp.broadcast_to(slopes[:, None, None], (ATTN_HEADS, 1, 2 * ATTN_BLOCK))
    dist, dist0 = _dist_tables()
    kv_rows = sum(d * (rows // d + ATTN_BLOCK) for _, d in PATTERNS)
    q_rows = sum(rows for _, d in PATTERNS if d > 1)

    def chunk_spec(ch):
        return pl.BlockSpec((None, rows, ATTN_STEP_W), lambda b, g, i: (ch, b * steps + i, g))

    tab_spec = pl.BlockSpec((ATTN_BLOCK, 2 * ATTN_BLOCK), lambda b, g, i: (0, 0))
    slope_spec = pl.BlockSpec((ATTN_HEADS_PER_STEP, 1, 2 * ATTN_BLOCK), lambda b, g, i: (g, 0, 0))
    return pl.pallas_call(
        _attn_kernel,
        grid=(batch, n_groups, steps),
        in_specs=[chunk_spec(CH_Q), chunk_spec(CH_K), chunk_spec(CH_V), chunk_spec(CH_ZA),
                  tab_spec, tab_spec, slope_spec],
        out_specs=pl.BlockSpec((rows, ATTN_STEP_W), lambda b, g, i: (b * steps + i, g)),
        out_shape=jax.ShapeDtypeStruct((t, ATTN_WIDTH), BF16),
        scratch_shapes=[
            pltpu.VMEM((2, ATTN_HEADS_PER_STEP, rows, LANES), F32),
            pltpu.VMEM((kv_rows, ATTN_STEP_W), BF16),
            pltpu.VMEM((kv_rows, ATTN_STEP_W), BF16),
            pltpu.VMEM((q_rows, ATTN_STEP_W), BF16),
            pltpu.VMEM((len(PATTERNS) * ATTN_HEADS_PER_STEP, rows, LANES), F32),
            pltpu.VMEM((len(PATTERNS) * ATTN_HEADS_PER_STEP, rows, LANES), F32),
            pltpu.VMEM((len(PATTERNS) * ATTN_HEADS_PER_STEP * 2, ATTN_BLOCK, 2 * ATTN_BLOCK), F32),
        ],
        compiler_params=_compiler_params(3, 48),
        name="attn",
    )(proj, proj, proj, proj, dist, dist0, slope_tab)


def _ssd_kernel(xs_ref, bc_ref, zs_ref, dt_ref, convw_ref, convb_ref, dtb_ref, alog_ref,
                dskip_ref, normw_ref, tril_ref, ws_ref, y_ref,
                xext, state, xs_s, b_s, c_s, acum_s, acumt_s, dt_s, xsc_s, y_s, ynbuf,
                *, ts, steps):
    L = SSM_CHUNK
    n_state = SSM_STATE
    n_slabs = SSM_CONV_DIM // LANES
    g_step = pl.program_id(0)
    cur = g_step % 2

    @pl.when(g_step == 0)
    def _():
        ynbuf[1] = jnp.zeros((ts, SSM_INNER), BF16)

    @pl.when(g_step % steps == 0)
    def _():
        state[...] = jnp.zeros_like(state)
        for s in range(n_slabs):
            xext[s, 0:8, :] = jnp.zeros((8, LANES), F32)

    row = lax.broadcasted_iota(jnp.int32, (L, L), 0)
    col = lax.broadcasted_iota(jnp.int32, (L, L), 1)
    causal = row >= col
    low_half = col < SSM_HEAD_DIM
    neg_a = -jnp.exp(alog_ref[...])

    def chunk_body(c, carry):
        r0 = pl.multiple_of(c * L, L)

        per_chunk = CHUNK_W // LANES
        for s in range(n_slabs):
            lo = s * LANES
            c0 = (s % per_chunk) * LANES
            if s < 2 * per_chunk:
                raw = xs_ref[s // per_chunk, pl.ds(r0, L), c0:c0 + LANES]
            else:
                raw = bc_ref[pl.ds(r0, L), c0:c0 + LANES]
            xext[s, 8:8 + L, :] = raw.astype(F32)
            acc = jnp.broadcast_to(convb_ref[:, lo:lo + LANES], (L, LANES))
            for k in range(SSM_CONV):
                shift = 8 - (SSM_CONV - 1) + k
                acc = acc + convw_ref[k:k + 1, lo:lo + LANES] * xext[s, shift:shift + L, :]
            xext[s, 0:8, :] = xext[s, L:L + 8, :]
            xc = acc * _sigmoid(acc)
            if lo < SSM_INNER:
                xs_s[:, lo:lo + LANES] = xc
            elif lo < SSM_INNER + SSM_GROUPS * n_state:
                b_s[:, lo - SSM_INNER:lo - SSM_INNER + LANES] = xc.astype(BF16)
            else:
                o2 = lo - SSM_INNER - SSM_GROUPS * n_state
                c_s[:, o2:o2 + LANES] = xc

        dt_in = dt_ref[pl.ds(r0, L), :] + dtb_ref[...]
        dtv = jnp.maximum(dt_in, 0.0) + jnp.log(1.0 + jnp.exp(-jnp.abs(dt_in)))
        da = dtv * neg_a
        acum = jnp.dot(tril_ref[...], da, preferred_element_type=F32,
                       precision=lax.Precision.HIGHEST)
        acum_s[...] = acum
        acumt_s[...] = acum.T
        dt_s[...] = dtv

        proj_w = D_MODEL // SSM_GROUPS
        for g in range(SSM_GROUPS):
            p0 = g * proj_w
            y_ref[pl.ds(r0, L), p0:p0 + proj_w] = jnp.dot(
                ynbuf[1 - cur, pl.ds(r0, L), :], ws_ref[:, p0:p0 + proj_w],
                preferred_element_type=F32).astype(BF16)

            n0 = g * n_state
            bg = b_s[:, n0:n0 + n_state]
            cg = c_s[:, n0:n0 + n_state]
            cb = lax.dot_general(cg.astype(BF16), bg, (((1,), (1,)), ((), ())),
                                 preferred_element_type=F32)
            elast_tiles = []
            for jp in range(HEADS_PER_GROUP // 2):
                col0 = (g * (HEADS_PER_GROUP // 2) + jp) * LANES
                lhs = []
                cols = []
                for e in range(2):
                    h = g * HEADS_PER_GROUP + jp * 2 + e
                    col_a = acum_s[:, h:h + 1]
                    row_a = acumt_s[h:h + 1, :]
                    last = acum_s[L - 1:L, h:h + 1]
                    seg = col_a - row_a
                    decay = jnp.exp(jnp.where(causal, seg, -jnp.inf))
                    m_mat = (cb * decay).astype(BF16)
                    c_scaled = (cg * jnp.exp(col_a)).astype(BF16)
                    lhs.append(jnp.concatenate([m_mat, c_scaled], axis=1))
                    cols.append((dt_s[:, h:h + 1], jnp.exp(last - col_a), jnp.exp(last)))
                dt_pair = jnp.where(low_half, cols[0][0], cols[1][0])
                w_pair = jnp.where(low_half, cols[0][1], cols[1][1])
                elast_tiles.append(jnp.where(low_half[0:1, :], cols[0][2], cols[1][2]))
                xs_pair = xs_s[:, col0:col0 + LANES]
                xdt = xs_pair * dt_pair
                xsc_s[:, col0:col0 + LANES] = (xdt * w_pair).astype(BF16)
                rhs = jnp.concatenate([xdt.astype(BF16), state[:, col0:col0 + LANES].astype(BF16)],
                                      axis=0)
                y0 = jnp.dot(lhs[0], rhs, preferred_element_type=F32)
                y1 = jnp.dot(lhs[1], rhs, preferred_element_type=F32)
                y_s[:, col0:col0 + LANES] = (jnp.where(low_half, y0, y1)
                                             + dskip_ref[:, col0:col0 + LANES] * xs_pair)
            g0 = g * HEADS_PER_GROUP * SSM_HEAD_DIM
            gw = HEADS_PER_GROUP * SSM_HEAD_DIM
            upd = lax.dot_general(bg, xsc_s[:, g0:g0 + gw], (((0,), (0,)), ((), ())),
                                  preferred_element_type=F32)
            elast = jnp.concatenate(elast_tiles, axis=1)
            state[:, g0:g0 + gw] = state[:, g0:g0 + gw] * elast + upd

        ssq = jnp.zeros((L, 1), F32)
        for hf in range(2):
            z = zs_ref[hf, pl.ds(r0, L), :].astype(F32)
            gy = y_s[:, hf * CHUNK_W:(hf + 1) * CHUNK_W] * (z * _sigmoid(z))
            y_s[:, hf * CHUNK_W:(hf + 1) * CHUNK_W] = gy
            ssq = ssq + jnp.sum(gy * gy, axis=-1, keepdims=True)
        inv = lax.rsqrt(ssq * (1.0 / SSM_INNER) + RMS_EPS)
        ynbuf[cur, pl.ds(r0, L), :] = (y_s[...] * inv * normw_ref[...]).astype(BF16)
        return carry

    lax.fori_loop(0, ts // L, chunk_body, 0)


def _ssd(proj, dt_raw, conv_w, conv_b, dt_bias, a_log, d_skip, ssm_norm_w, w_s, batch, seq):
    t = batch * seq
    ts = min(256, seq)
    steps = seq // ts
    n_steps = batch * steps
    last = n_steps - 1
    pad = LANES - SSM_HEADS
    dtb = jnp.pad(dt_bias.astype(F32), (0, pad)).reshape(1, LANES)
    alog = jnp.pad(a_log.astype(F32), (0, pad)).reshape(1, LANES)
    dskip = jnp.repeat(d_skip.astype(F32), SSM_HEAD_DIM).reshape(1, SSM_INNER)
    tril = jnp.tril(jnp.ones((SSM_CHUNK, SSM_CHUNK), F32))

    def full(shape, **kw):
        return pl.BlockSpec(shape, lambda g: (0,) * len(shape), **kw)

    def rows_of(g):
        return jnp.minimum(g, last)

    return pl.pallas_call(
        functools.partial(_ssd_kernel, ts=ts, steps=steps),
        grid=(n_steps + 1,),
        in_specs=[
            pl.BlockSpec((2, ts, CHUNK_W), lambda g: (CH_XS // 2, rows_of(g), 0)),
            pl.BlockSpec((None, ts, CHUNK_W), lambda g: (CH_BC, rows_of(g), 0)),
            pl.BlockSpec((2, ts, CHUNK_W), lambda g: (CH_ZS // 2, rows_of(g), 0)),
            pl.BlockSpec((ts, LANES), lambda g: (rows_of(g), 0)),
            full((SSM_CONV, SSM_CONV_DIM)),
            full((1, SSM_CONV_DIM)),
            full((1, LANES)),
            full((1, LANES)),
            full((1, SSM_INNER)),
            full((1, SSM_INNER)),
            full((SSM_CHUNK, SSM_CHUNK)),
            full((SSM_INNER, D_MODEL), pipeline_mode=pl.Buffered(1)),
        ],
        out_specs=pl.BlockSpec((ts, D_MODEL), lambda g: (jnp.maximum(g - 1, 0), 0)),
        out_shape=jax.ShapeDtypeStruct((t, D_MODEL), BF16),
        scratch_shapes=[
            pltpu.VMEM((SSM_CONV_DIM // LANES, SSM_CHUNK + 8, LANES), F32),
            pltpu.VMEM((SSM_STATE, SSM_INNER), F32),
            pltpu.VMEM((SSM_CHUNK, SSM_INNER), F32),
            pltpu.VMEM((SSM_CHUNK, SSM_GROUPS * SSM_STATE), BF16),
            pltpu.VMEM((SSM_CHUNK, SSM_GROUPS * SSM_STATE), F32),
            pltpu.VMEM((SSM_CHUNK, LANES), F32),
            pltpu.VMEM((LANES, SSM_CHUNK), F32),
            pltpu.VMEM((SSM_CHUNK, LANES), F32),
            pltpu.VMEM((SSM_CHUNK, SSM_INNER), BF16),
            pltpu.VMEM((SSM_CHUNK, SSM_INNER), F32),
            pltpu.VMEM((2, ts, SSM_INNER), BF16),
        ],
        compiler_params=_compiler_params(1, 56),
        name="ssd",
    )(proj, proj, proj, dt_raw, conv_w.astype(F32), conv_b.astype(F32).reshape(1, SSM_CONV_DIM),
      dtb, alog, dskip, ssm_norm_w.astype(F32).reshape(1, SSM_INNER), tril, w_s)


def _tail_kernel(oa_ref, ys_ref, ga_ref, gs_ref, x_ref, wa_ref, wo_ref, nw_ref, o_ref):
    a = jnp.dot(oa_ref[...], wa_ref[...], preferred_element_type=F32)
    ga = _sigmoid(ga_ref[...].astype(F32))
    gs = _sigmoid(gs_ref[...].astype(F32))
    merged = (ga * a + gs * ys_ref[...].astype(F32)).astype(BF16)
    r = x_ref[...] + jnp.dot(merged, wo_ref[...], preferred_element_type=F32)
    ms = jnp.mean(r * r, axis=-1, keepdims=True)
    o_ref[...] = r * lax.rsqrt(ms + RMS_EPS) * nw_ref[...]


def _tail(o_a, y_s, proj, x2, w_a, w_out, final_norm_w):
    t = x2.shape[0]
    tm = min(256, t)

    def rows(shape):
        return pl.BlockSpec(shape, lambda i: (i, 0))

    def resident(shape):
        return pl.BlockSpec(shape, lambda i: (0, 0), pipeline_mode=pl.Buffered(1))

    return pl.pallas_call(
        _tail_kernel,
        grid=(t // tm,),
        in_specs=[
            rows((tm, ATTN_WIDTH)),
            rows((tm, D_MODEL)),
            pl.BlockSpec((None, tm, CHUNK_W), lambda i: (CH_GA, i, 0)),
            pl.BlockSpec((None, tm, CHUNK_W), lambda i: (CH_GS, i, 0)),
            rows((tm, D_MODEL)),
            resident((ATTN_WIDTH, D_MODEL)),
            resident((D_MODEL, D_MODEL)),
            resident((1, D_MODEL)),
        ],
        out_specs=rows((tm, D_MODEL)),
        out_shape=jax.ShapeDtypeStruct((t, D_MODEL), F32),
        compiler_params=_compiler_params(1, 48),
        name="tail",
    )(o_a, y_s, proj, proj, x2, w_a, w_out, final_norm_w)


def _layer(x, norm_w, w_in, conv_w, conv_b, dt_bias, a_log, d_skip, ssm_norm_w,
           w_attn_branch, w_ssm_branch, w_out, out_norm_w):
    batch, seq, _ = x.shape
    t = batch * seq
    x2 = x.reshape(t, D_MODEL)

    w_all = w_in.astype(BF16)
    w_gates = w_all[:, DT_COL0 + SSM_HEADS:]
    w_dt = jnp.pad(w_all[:, DT_COL0:DT_COL0 + SSM_HEADS], ((0, 0), (0, LANES - SSM_HEADS)))
    proj, dt_raw = _in_proj(x2, norm_w.astype(F32).reshape(1, D_MODEL), w_all, w_gates, w_dt)

    o_a = _attention(proj, batch, seq)
    y_s = _ssd(proj, dt_raw, conv_w, conv_b, dt_bias, a_log, d_skip, ssm_norm_w,
               w_ssm_branch.astype(BF16), batch, seq)
    out = _tail(o_a, y_s, proj, x2, w_attn_branch.astype(BF16), w_out.astype(BF16),
                out_norm_w.astype(F32).reshape(1, D_MODEL))
    return out.reshape(batch, seq, D_MODEL)


def kernel(x, norm_w, w_in, conv_w, conv_b, dt_bias, a_log, d_skip, ssm_norm_w,
           w_attn_branch, w_ssm_branch, w_out, final_norm_w):
    depth = norm_w.shape[0]
    assert depth == 1, "the residual stream between layers is not normalised; only depth 1 is fused"
    assert x.shape[1] % ATTN_ROWS == 0 and x.shape[2] == D_MODEL
    return _layer(x, norm_w[0], w_in[0], conv_w[0], conv_b[0], dt_bias[0], a_log[0], d_skip[0],
                  ssm_norm_w[0], w_attn_branch[0], w_ssm_branch[0], w_out[0], final_norm_w)
```

```python
import functools

import jax
import jax.numpy as jnp
from jax import lax
from jax.experimental import pallas as pl
from jax.experimental.pallas import tpu as pltpu

D_MODEL = 2048
ATTN_HEADS = 16
HEAD_DIM = 128
ATTN_WIDTH = ATTN_HEADS * HEAD_DIM
PATTERNS = ((128, 1), (512, 4), (2048, 16))
ATTN_BLOCK = 128

SSM_INNER = 4096
SSM_HEAD_DIM = 64
SSM_HEADS = 64
SSM_GROUPS = 8
SSM_STATE = 128
SSM_CONV = 4
SSM_CHUNK = 128
SSM_CONV_DIM = SSM_INNER + 2 * SSM_GROUPS * SSM_STATE
HEADS_PER_GROUP = SSM_HEADS // SSM_GROUPS
RMS_EPS = 1e-6
LOG2E = 1.4426950408889634

LANES = 128
CHUNK_W = D_MODEL
CH_Q, CH_K, CH_V, CH_ZA, CH_ZS, CH_XS, CH_BC, CH_GA, CH_GS = 0, 1, 2, 3, 4, 6, 8, 9, 10
N_CHUNKS = 11
DT_COL0 = 4 * ATTN_WIDTH + SSM_INNER + SSM_CONV_DIM

F32 = jnp.float32
BF16 = jnp.bfloat16
MIB = 1024 * 1024


def _sigmoid(v):
    return 0.5 + 0.5 * jnp.tanh(0.5 * v)


def _silu(v):
    h = 0.5 * v
    return h + h * jnp.tanh(h)


def _compiler_params(n_grid, vmem_mib):
    return pltpu.CompilerParams(
        dimension_semantics=("arbitrary",) * n_grid,
        vmem_limit_bytes=vmem_mib * MIB,
    )


def _in_proj_kernel(x_ref, nw_ref, w_ref, wg_ref, wdt_ref, o_ref, dt_ref, hn_ref, *, n_lead):
    j = pl.program_id(1)

    @pl.when(j == 0)
    def _():
        x = x_ref[...]
        ms = jnp.mean(x * x, axis=-1, keepdims=True)
        hn = (x * lax.rsqrt(ms + RMS_EPS) * nw_ref[...]).astype(BF16)
        hn_ref[...] = hn
        dt_ref[...] = jnp.dot(hn, wdt_ref[...], preferred_element_type=F32)

    @pl.when(j < n_lead)
    def _():
        o_ref[...] = jnp.dot(hn_ref[...], w_ref[...], preferred_element_type=F32).astype(BF16)

    @pl.when(j >= n_lead)
    def _():
        o_ref[...] = jnp.dot(hn_ref[...], wg_ref[...], preferred_element_type=F32).astype(BF16)


def _in_proj(x2, norm_w, w_all, w_gates, w_dt):
    t = x2.shape[0]
    tm = min(1024, t)
    tn = 1024
    per = CHUNK_W // tn
    n_lead = DT_COL0 // tn
    return pl.pallas_call(
        functools.partial(_in_proj_kernel, n_lead=n_lead),
        grid=(t // tm, N_CHUNKS * per),
        in_specs=[
            pl.BlockSpec((tm, D_MODEL), lambda i, j: (i, 0)),
            pl.BlockSpec((1, D_MODEL), lambda i, j: (0, 0)),
            pl.BlockSpec((D_MODEL, tn), lambda i, j: (0, jnp.minimum(j, n_lead - 1))),
            pl.BlockSpec((D_MODEL, tn), lambda i, j: (0, jnp.maximum(j - n_lead, 0))),
            pl.BlockSpec((D_MODEL, LANES), lambda i, j: (0, 0)),
        ],
        out_specs=[
            pl.BlockSpec((None, tm, tn), lambda i, j: (j // per, i, j % per)),
            pl.BlockSpec((tm, LANES), lambda i, j: (i, 0)),
        ],
        out_shape=[
            jax.ShapeDtypeStruct((N_CHUNKS, t, CHUNK_W), BF16),
            jax.ShapeDtypeStruct((t, LANES), F32),
        ],
        scratch_shapes=[pltpu.VMEM((tm, D_MODEL), BF16)],
        compiler_params=_compiler_params(2, 48),
        name="in_proj",
    )(x2, norm_w, w_all, w_gates, w_dt)


ATTN_ROWS = PATTERNS[-1][1] * ATTN_BLOCK
ATTN_HEADS_PER_STEP = 2
ATTN_STEP_W = ATTN_HEADS_PER_STEP * HEAD_DIM
ATTN_UNROLL = 8


def _attn_kernel(q_ref, k_ref, v_ref, z_ref, dist_ref, dist0_ref, slope_ref, o_ref,
                 slabs, kd, vd, qd, oacc, lacc, bias):
    i = pl.program_id(2)
    blk = ATTN_BLOCK
    rows = ATTN_ROWS
    nh = ATTN_HEADS_PER_STEP
    scale = HEAD_DIM ** -0.5

    geo = []
    kv_base = q_base = 0
    for _, dil in PATTERNS:
        sub = rows // dil
        geo.append((dil, sub, kv_base, sub + blk, q_base))
        kv_base += dil * (sub + blk)
        q_base += rows if dil > 1 else 0

    prev_rows = [(base + r * pitch, sub) for dil, sub, base, pitch, _ in geo for r in range(dil)]

    @pl.when(i == 0)
    def _():
        for p0, _ in prev_rows:
            kd[p0:p0 + blk, :] = jnp.zeros((blk, ATTN_STEP_W), BF16)
            vd[p0:p0 + blk, :] = jnp.zeros((blk, ATTN_STEP_W), BF16)

    @pl.when(i != 0)
    def _():
        for p0, sub in prev_rows:
            kd[p0:p0 + blk, :] = kd[p0 + sub:p0 + sub + blk, :]
            vd[p0:p0 + blk, :] = vd[p0 + sub:p0 + sub + blk, :]

    def regroup(src_ref, dst, is_kv):
        def stage(rb, carry):
            rws = pl.ds(pl.multiple_of(rb * blk, blk), blk)
            for s in range(nh):
                slabs[0, s, rws, :] = src_ref[rws, s * LANES:(s + 1) * LANES].astype(F32)
            return carry

        lax.fori_loop(0, rows // blk, stage, 0, unroll=4)
        level, d_prev, sub_prev = 0, 1, rows
        for li, (dil, sub, base, pitch, qb) in enumerate(geo):
            nblk = sub // blk
            if dil == 1:
                if is_kv:
                    def copy(rb, carry, base=base):
                        r0 = pl.multiple_of(rb * blk, blk)
                        dst[pl.ds(base + blk + r0, blk), :] = src_ref[pl.ds(r0, blk), :]
                        return carry

                    lax.fori_loop(0, nblk, copy, 0, unroll=4)
                continue
            ratio = dil // d_prev
            keep = li < len(geo) - 1

            def gather(it, carry, sub=sub, base=base, pitch=pitch, qb=qb, nblk=nblk,
                       ratio=ratio, d_prev=d_prev, sub_prev=sub_prev, level=level, keep=keep):
                r = it // nblk
                b = it % nblk
                src0 = (r % d_prev) * sub_prev + r // d_prev + b * blk * ratio
                p0 = base + r * pitch + blk if is_kv else qb + r * sub
                p0 = pl.multiple_of(p0 + b * blk, blk)
                for s in range(nh):
                    val = slabs[level, s, pl.ds(src0, blk, stride=ratio), :]
                    dst[pl.ds(p0, blk), s * LANES:(s + 1) * LANES] = val.astype(BF16)
                    if keep:
                        slabs[1 - level, s, pl.ds(pl.multiple_of(r * sub + b * blk, blk), blk), :] = val
                return carry

            lax.fori_loop(0, dil * nblk, gather, 0, unroll=4)
            level, d_prev, sub_prev = 1 - level, dil, sub

    regroup(k_ref, kd, True)
    regroup(v_ref, vd, True)
    regroup(q_ref, qd, False)

    for pi, (dil, _, _, _, _) in enumerate(geo):
        for s in range(nh):
            slope = slope_ref[s] * (float(dil) * LOG2E)
            bias[(pi * nh + s) * 2] = slope * dist_ref[...]
            bias[(pi * nh + s) * 2 + 1] = slope * dist0_ref[...]

    ones = jnp.ones((2 * blk, LANES), BF16)

    def unit(q, kc, vc, bias_t):
        s = lax.dot_general(q, kc, (((1,), (1,)), ((), ())), preferred_element_type=F32)
        s = s * (scale * LOG2E) - bias_t
        m = jnp.max(s, axis=-1, keepdims=True)
        p = jnp.exp2(s - m)
        pv = jnp.dot(p.astype(BF16), jnp.concatenate([vc, ones], axis=1),
                     preferred_element_type=F32)
        l_b = pv[:, LANES:]
        m_b = jnp.broadcast_to(m, (blk, LANES))
        return pv[:, :LANES] * (1.0 / l_b), m_b + jnp.log(l_b) * LOG2E

    for pi, (dil, sub, base, pitch, qb) in enumerate(geo):
        nblk = sub // blk

        def body(it, carry, dil=dil, sub=sub, base=base, pitch=pitch, qb=qb, nblk=nblk, pi=pi):
            r = it // nblk
            b = it % nblk
            no_prev = jnp.logical_and(i == 0, b == 0).astype(jnp.int32)
            q0 = pl.multiple_of(qb + r * sub + b * blk, blk)
            k0 = pl.multiple_of(base + r * pitch + b * blk, blk)
            t0 = r + b * blk * dil
            if dil == 1:
                tok = pl.ds(pl.multiple_of(t0, blk), blk)
            else:
                tok = pl.ds(t0, blk, stride=dil)
            for s in range(nh):
                c0 = s * LANES
                q_src = q_ref if dil == 1 else qd
                o, lse = unit(q_src[pl.ds(q0, blk), c0:c0 + LANES],
                              kd[pl.ds(k0, 2 * blk), c0:c0 + LANES],
                              vd[pl.ds(k0, 2 * blk), c0:c0 + LANES],
                              bias[(pi * nh + s) * 2 + no_prev])
                oacc[pi * nh + s, tok, :] = o
                lacc[pi * nh + s, tok, :] = lse
            return carry

        lax.fori_loop(0, dil * nblk, body, 0, unroll=ATTN_UNROLL)

    n_pat = len(geo)

    def mix(bi, carry):
        tok = pl.ds(pl.multiple_of(bi * blk, blk), blk)
        for s in range(nh):
            lses = [lacc[pi * nh + s, tok, :] for pi in range(n_pat)]
            top = functools.reduce(jnp.maximum, lses)
            ws = [jnp.exp2(l - top) for l in lses]
            num = sum(w * oacc[pi * nh + s, tok, :] for pi, w in enumerate(ws))
            z = z_ref[tok, s * LANES:(s + 1) * LANES].astype(F32)
            o_ref[tok, s * LANES:(s + 1) * LANES] = (
                num * (1.0 / sum(ws)) * _silu(z)).astype(BF16)
        return carry

    lax.fori_loop(0, rows // blk, mix, 0, unroll=4)


def _dist_tables():
    blk = ATTN_BLOCK
    qi = jnp.arange(blk)[:, None]
    ki = jnp.arange(2 * blk)[None, :]
    dist = qi - ki + blk
    valid = (dist >= 0) & (dist <= blk)
    d = jnp.where(valid, dist.astype(F32), jnp.inf)
    d0 = jnp.where(ki >= blk, d, jnp.inf)
    return d, d0


def _attention(proj, batch, seq):
    assert all(w // d == ATTN_BLOCK for w, d in PATTERNS)
    t = batch * seq
    rows = ATTN_ROWS
    steps = seq // rows
    n_groups = ATTN_HEADS // ATTN_HEADS_PER_STEP
    slopes = jnp.asarray([2.0 ** (-8.0 * (h + 1) / ATTN_HEADS) for h in range(ATTN_HEADS)], F32)
    slope_tab = jnp.broadcast_to(slopes[:, None, None], (ATTN_HEADS, 1, 2 * ATTN_BLOCK))
    dist, dist0 = _dist_tables()
    kv_rows = sum(d * (rows // d + ATTN_BLOCK) for _, d in PATTERNS)
    q_rows = sum(rows for _, d in PATTERNS if d > 1)

    def chunk_spec(ch):
        return pl.BlockSpec((None, rows, ATTN_STEP_W), lambda b, g, i: (ch, b * steps + i, g))

    tab_spec = pl.BlockSpec((ATTN_BLOCK, 2 * ATTN_BLOCK), lambda b, g, i: (0, 0))
    slope_spec = pl.BlockSpec((ATTN_HEADS_PER_STEP, 1, 2 * ATTN_BLOCK), lambda b, g, i: (g, 0, 0))
    return pl.pallas_call(
        _attn_kernel,
        grid=(batch, n_groups, steps),
        in_specs=[chunk_spec(CH_Q), chunk_spec(CH_K), chunk_spec(CH_V), chunk_spec(CH_ZA),
                  tab_spec, tab_spec, slope_spec],
        out_specs=pl.BlockSpec((rows, ATTN_STEP_W), lambda b, g, i: (b * steps + i, g)),
        out_shape=jax.ShapeDtypeStruct((t, ATTN_WIDTH), BF16),
        scratch_shapes=[
            pltpu.VMEM((2, ATTN_HEADS_PER_STEP, rows, LANES), F32),
            pltpu.VMEM((kv_rows, ATTN_STEP_W), BF16),
            pltpu.VMEM((kv_rows, ATTN_STEP_W), BF16),
            pltpu.VMEM((q_rows, ATTN_STEP_W), BF16),
            pltpu.VMEM((len(PATTERNS) * ATTN_HEADS_PER_STEP, rows, LANES), F32),
            pltpu.VMEM((len(PATTERNS) * ATTN_HEADS_PER_STEP, rows, LANES), F32),
            pltpu.VMEM((len(PATTERNS) * ATTN_HEADS_PER_STEP * 2, ATTN_BLOCK, 2 * ATTN_BLOCK), F32),
        ],
        compiler_params=_compiler_params(3, 48),
        name="attn",
    )(proj, proj, proj, proj, dist, dist0, slope_tab)


def _ssd_kernel(xs_ref, bc_ref, zs_ref, dt_ref, convw_ref, convb_ref, dtb_ref, alog_ref,
                dskip_ref, normw_ref, tril_ref, ws_ref, y_ref,
                xext, state, xs_s, b_s, c_s, acum_s, acumt_s, dt_s, xsc_s, y_s, ynbuf,
                *, ts, steps):
    L = SSM_CHUNK
    n_state = SSM_STATE
    n_slabs = SSM_CONV_DIM // LANES
    g_step = pl.program_id(0)
    cur = g_step % 2

    @pl.when(g_step == 0)
    def _():
        ynbuf[1] = jnp.zeros((ts, SSM_INNER), BF16)

    @pl.when(g_step % steps == 0)
    def _():
        state[...] = jnp.zeros_like(state)
        for s in range(n_slabs):
            xext[s, 0:8, :] = jnp.zeros((8, LANES), F32)

    row = lax.broadcasted_iota(jnp.int32, (L, L), 0)
    col = lax.broadcasted_iota(jnp.int32, (L, L), 1)
    causal = row >= col
    low_half = col < SSM_HEAD_DIM
    neg_a = -jnp.exp(alog_ref[...])

    def chunk_body(c, carry):
        r0 = pl.multiple_of(c * L, L)

        per_chunk = CHUNK_W // LANES
        for s in range(n_slabs):
            lo = s * LANES
            c0 = (s % per_chunk) * LANES
            if s < 2 * per_chunk:
                raw = xs_ref[s // per_chunk, pl.ds(r0, L), c0:c0 + LANES]
            else:
                raw = bc_ref[pl.ds(r0, L), c0:c0 + LANES]
            xext[s, 8:8 + L, :] = raw.astype(F32)
            acc = jnp.broadcast_to(convb_ref[:, lo:lo + LANES], (L, LANES))
            for k in range(SSM_CONV):
                shift = 8 - (SSM_CONV - 1) + k
                acc = acc + convw_ref[k:k + 1, lo:lo + LANES] * xext[s, shift:shift + L, :]
            xext[s, 0:8, :] = xext[s, L:L + 8, :]
            xc = _silu(acc)
            if lo < SSM_INNER:
                xs_s[:, lo:lo + LANES] = xc
            elif lo < SSM_INNER + SSM_GROUPS * n_state:
                b_s[:, lo - SSM_INNER:lo - SSM_INNER + LANES] = xc.astype(BF16)
            else:
                o2 = lo - SSM_INNER - SSM_GROUPS * n_state
                c_s[:, o2:o2 + LANES] = xc

        dt_in = dt_ref[pl.ds(r0, L), :] + dtb_ref[...]
        dtv = jnp.maximum(dt_in, 0.0) + jnp.log(1.0 + jnp.exp(-jnp.abs(dt_in)))
        da = dtv * neg_a
        acum = jnp.dot(tril_ref[...], da, preferred_element_type=F32,
                       precision=lax.Precision.HIGHEST)
        acum_s[...] = acum
        acumt_s[...] = acum.T
        dt_s[...] = dtv

        proj_w = D_MODEL // SSM_GROUPS
        for g in range(SSM_GROUPS):
            p0 = g * proj_w
            y_ref[pl.ds(r0, L), p0:p0 + proj_w] = jnp.dot(
                ynbuf[1 - cur, pl.ds(r0, L), :], ws_ref[:, p0:p0 + proj_w],
                preferred_element_type=F32).astype(BF16)

            n0 = g * n_state
            bg = b_s[:, n0:n0 + n_state]
            cg = c_s[:, n0:n0 + n_state]
            cb = lax.dot_general(cg.astype(BF16), bg, (((1,), (1,)), ((), ())),
                                 preferred_element_type=F32)
            cb = jnp.where(causal, cb, 0.0)
            elast_tiles = []
            for jp in range(HEADS_PER_GROUP // 2):
                col0 = (g * (HEADS_PER_GROUP // 2) + jp) * LANES
                lhs = []
                cols = []
                for e in range(2):
                    h = g * HEADS_PER_GROUP + jp * 2 + e
                    col_a = acum_s[:, h:h + 1]
                    row_a = acumt_s[h:h + 1, :]
                    last = acum_s[L - 1:L, h:h + 1]
                    seg = col_a - row_a
                    decay = jnp.exp(jnp.minimum(seg, 0.0))
                    m_mat = (cb * decay).astype(BF16)
                    c_scaled = (cg * jnp.exp(col_a)).astype(BF16)
                    lhs.append(jnp.concatenate([m_mat, c_scaled], axis=1))
                    cols.append((dt_s[:, h:h + 1], jnp.exp(last - col_a), jnp.exp(last)))
                dt_pair = jnp.where(low_half, cols[0][0], cols[1][0])
                w_pair = jnp.where(low_half, cols[0][1], cols[1][1])
                elast_tiles.append(jnp.where(low_half[0:1, :], cols[0][2], cols[1][2]))
                xs_pair = xs_s[:, col0:col0 + LANES]
                xdt = xs_pair * dt_pair
                xsc_s[:, col0:col0 + LANES] = (xdt * w_pair).astype(BF16)
                rhs = jnp.concatenate([xdt.astype(BF16), state[:, col0:col0 + LANES].astype(BF16)],
                                      axis=0)
                y0 = jnp.dot(lhs[0], rhs, preferred_element_type=F32)
                y1 = jnp.dot(lhs[1], rhs, preferred_element_type=F32)
                y_s[:, col0:col0 + LANES] = (jnp.where(low_half, y0, y1)
                                             + dskip_ref[:, col0:col0 + LANES] * xs_pair)
            g0 = g * HEADS_PER_GROUP * SSM_HEAD_DIM
            gw = HEADS_PER_GROUP * SSM_HEAD_DIM
            upd = lax.dot_general(bg, xsc_s[:, g0:g0 + gw], (((0,), (0,)), ((), ())),
                                  preferred_element_type=F32)
            elast = jnp.concatenate(elast_tiles, axis=1)
            state[:, g0:g0 + gw] = state[:, g0:g0 + gw] * elast + upd

        ssq = jnp.zeros((L, 1), F32)
        for hf in range(2):
            z = zs_ref[hf, pl.ds(r0, L), :].astype(F32)
            gy = y_s[:, hf * CHUNK_W:(hf + 1) * CHUNK_W] * _silu(z)
            y_s[:, hf * CHUNK_W:(hf + 1) * CHUNK_W] = gy
            ssq = ssq + jnp.sum(gy * gy, axis=-1, keepdims=True)
        inv = lax.rsqrt(ssq * (1.0 / SSM_INNER) + RMS_EPS)
        ynbuf[cur, pl.ds(r0, L), :] = (y_s[...] * inv * normw_ref[...]).astype(BF16)
        return carry

    lax.fori_loop(0, ts // L, chunk_body, 0)


def _ssd(proj, dt_raw, conv_w, conv_b, dt_bias, a_log, d_skip, ssm_norm_w, w_s, batch, seq):
    t = batch * seq
    ts = min(256, seq)
    steps = seq // ts
    n_steps = batch * steps
    last = n_steps - 1
    pad = LANES - SSM_HEADS
    dtb = jnp.pad(dt_bias.astype(F32), (0, pad)).reshape(1, LANES)
    alog = jnp.pad(a_log.astype(F32), (0, pad)).reshape(1, LANES)
    dskip = jnp.repeat(d_skip.astype(F32), SSM_HEAD_DIM).reshape(1, SSM_INNER)
    tril = jnp.tril(jnp.ones((SSM_CHUNK, SSM_CHUNK), F32))

    def full(shape, **kw):
        return pl.BlockSpec(shape, lambda g: (0,) * len(shape), **kw)

    def rows_of(g):
        return jnp.minimum(g, last)

    return pl.pallas_call(
        functools.partial(_ssd_kernel, ts=ts, steps=steps),
        grid=(n_steps + 1,),
        in_specs=[
            pl.BlockSpec((2, ts, CHUNK_W), lambda g: (CH_XS // 2, rows_of(g), 0)),
            pl.BlockSpec((None, ts, CHUNK_W), lambda g: (CH_BC, rows_of(g), 0)),
            pl.BlockSpec((2, ts, CHUNK_W), lambda g: (CH_ZS // 2, rows_of(g), 0)),
            pl.BlockSpec((ts, LANES), lambda g: (rows_of(g), 0)),
            full((SSM_CONV, SSM_CONV_DIM)),
            full((1, SSM_CONV_DIM)),
            full((1, LANES)),
            full((1, LANES)),
            full((1, SSM_INNER)),
            full((1, SSM_INNER)),
            full((SSM_CHUNK, SSM_CHUNK)),
            full((SSM_INNER, D_MODEL), pipeline_mode=pl.Buffered(1)),
        ],
        out_specs=pl.BlockSpec((ts, D_MODEL), lambda g: (jnp.maximum(g - 1, 0), 0)),
        out_shape=jax.ShapeDtypeStruct((t, D_MODEL), BF16),
        scratch_shapes=[
            pltpu.VMEM((SSM_CONV_DIM // LANES, SSM_CHUNK + 8, LANES), F32),
            pltpu.VMEM((SSM_STATE, SSM_INNER), F32),
            pltpu.VMEM((SSM_CHUNK, SSM_INNER), F32),
            pltpu.VMEM((SSM_CHUNK, SSM_GROUPS * SSM_STATE), BF16),
            pltpu.VMEM((SSM_CHUNK, SSM_GROUPS * SSM_STATE), F32),
            pltpu.VMEM((SSM_CHUNK, LANES), F32),
            pltpu.VMEM((LANES, SSM_CHUNK), F32),
            pltpu.VMEM((SSM_CHUNK, LANES), F32),
            pltpu.VMEM((SSM_CHUNK, SSM_INNER), BF16),
            pltpu.VMEM((SSM_CHUNK, SSM_INNER), F32),
            pltpu.VMEM((2, ts, SSM_INNER), BF16),
        ],
        compiler_params=_compiler_params(1, 56),
        name="ssd",
    )(proj, proj, proj, dt_raw, conv_w.astype(F32), conv_b.astype(F32).reshape(1, SSM_CONV_DIM),
      dtb, alog, dskip, ssm_norm_w.astype(F32).reshape(1, SSM_INNER), tril, w_s)


def _tail_kernel(oa_ref, ys_ref, ga_ref, gs_ref, x_ref, wa_ref, wo_ref, nw_ref, o_ref):
    a = jnp.dot(oa_ref[...], wa_ref[...], preferred_element_type=F32)
    ga = _sigmoid(ga_ref[...].astype(F32))
    gs = _sigmoid(gs_ref[...].astype(F32))
    merged = (ga * a + gs * ys_ref[...].astype(F32)).astype(BF16)
    r = x_ref[...] + jnp.dot(merged, wo_ref[...], preferred_element_type=F32)
    ms = jnp.mean(r * r, axis=-1, keepdims=True)
    o_ref[...] = r * lax.rsqrt(ms + RMS_EPS) * nw_ref[...]


def _tail(o_a, y_s, proj, x2, w_a, w_out, final_norm_w):
    t = x2.shape[0]
    tm = min(256, t)

    def rows(shape):
        return pl.BlockSpec(shape, lambda i: (i, 0))

    def resident(shape):
        return pl.BlockSpec(shape, lambda i: (0, 0), pipeline_mode=pl.Buffered(1))

    return pl.pallas_call(
        _tail_kernel,
        grid=(t // tm,),
        in_specs=[
            rows((tm, ATTN_WIDTH)),
            rows((tm, D_MODEL)),
            pl.BlockSpec((None, tm, CHUNK_W), lambda i: (CH_GA, i, 0)),
            pl.BlockSpec((None, tm, CHUNK_W), lambda i: (CH_GS, i, 0)),
            rows((tm, D_MODEL)),
            resident((ATTN_WIDTH, D_MODEL)),
            resident((D_MODEL, D_MODEL)),
            resident((1, D_MODEL)),
        ],
        out_specs=rows((tm, D_MODEL)),
        out_shape=jax.ShapeDtypeStruct((t, D_MODEL), F32),
        compiler_params=_compiler_params(1, 48),
        name="tail",
    )(o_a, y_s, proj, proj, x2, w_a, w_out, final_norm_w)


def _layer(x, norm_w, w_in, conv_w, conv_b, dt_bias, a_log, d_skip, ssm_norm_w,
           w_attn_branch, w_ssm_branch, w_out, out_norm_w):
    batch, seq, _ = x.shape
    t = batch * seq
    x2 = x.reshape(t, D_MODEL)

    w_all = w_in.astype(BF16)
    w_gates = w_all[:, DT_COL0 + SSM_HEADS:]
    w_dt = jnp.pad(w_all[:, DT_COL0:DT_COL0 + SSM_HEADS], ((0, 0), (0, LANES - SSM_HEADS)))
    proj, dt_raw = _in_proj(x2, norm_w.astype(F32).reshape(1, D_MODEL), w_all, w_gates, w_dt)

    o_a = _attention(proj, batch, seq)
    y_s = _ssd(proj, dt_raw, conv_w, conv_b, dt_bias, a_log, d_skip, ssm_norm_w,
               w_ssm_branch.astype(BF16), batch, seq)
    out = _tail(o_a, y_s, proj, x2, w_attn_branch.astype(BF16), w_out.astype(BF16),
                out_norm_w.astype(F32).reshape(1, D_MODEL))
    return out.reshape(batch, seq, D_MODEL)


def kernel(x, norm_w, w_in, conv_w, conv_b, dt_bias, a_log, d_skip, ssm_norm_w,
           w_attn_branch, w_ssm_branch, w_out, final_norm_w):
    depth = norm_w.shape[0]
    assert depth == 1, "the residual stream between layers is not normalised; only depth 1 is fused"
    assert x.shape[1] % ATTN_ROWS == 0 and x.shape[2] == D_MODEL
    return _layer(x, norm_w[0], w_in[0], conv_w[0], conv_b[0], dt_bias[0], a_log[0], d_skip[0],
                  ssm_norm_w[0], w_attn_branch[0], w_ssm_branch[0], w_out[0], final_norm_w)
```

```python
import functools

import jax
import jax.numpy as jnp
from jax import lax
from jax.experimental import pallas as pl
from jax.experimental.pallas import tpu as pltpu

D_MODEL = 2048
ATTN_HEADS = 16
HEAD_DIM = 128
ATTN_WIDTH = ATTN_HEADS * HEAD_DIM
PATTERNS = ((128, 1), (512, 4), (2048, 16))
ATTN_BLOCK = 128

SSM_INNER = 4096
SSM_HEAD_DIM = 64
SSM_HEADS = 64
SSM_GROUPS = 8
SSM_STATE = 128
SSM_CONV = 4
SSM_CHUNK = 128
SSM_CONV_DIM = SSM_INNER + 2 * SSM_GROUPS * SSM_STATE
HEADS_PER_GROUP = SSM_HEADS // SSM_GROUPS
RMS_EPS = 1e-6
LOG2E = 1.4426950408889634

LANES = 128
CHUNK_W = D_MODEL
CH_Q, CH_K, CH_V, CH_ZA, CH_ZS, CH_XS, CH_BC, CH_GA, CH_GS = 0, 1, 2, 3, 4, 6, 8, 9, 10
N_CHUNKS = 11
DT_COL0 = 4 * ATTN_WIDTH + SSM_INNER + SSM_CONV_DIM

F32 = jnp.float32
BF16 = jnp.bfloat16
MIB = 1024 * 1024


def _sigmoid(v):
    return 0.5 + 0.5 * jnp.tanh(0.5 * v)


def _silu(v):
    h = 0.5 * v
    return h + h * jnp.tanh(h)


def _compiler_params(n_grid, vmem_mib):
    return pltpu.CompilerParams(
        dimension_semantics=("arbitrary",) * n_grid,
        vmem_limit_bytes=vmem_mib * MIB,
    )


def _in_proj_kernel(x_ref, nw_ref, w_ref, wg_ref, wdt_ref, o_ref, dt_ref, hn_ref, *, n_lead):
    j = pl.program_id(1)

    @pl.when(j == 0)
    def _():
        x = x_ref[...]
        ms = jnp.mean(x * x, axis=-1, keepdims=True)
        hn = (x * lax.rsqrt(ms + RMS_EPS) * nw_ref[...]).astype(BF16)
        hn_ref[...] = hn
        dt_ref[...] = jnp.dot(hn, wdt_ref[...], preferred_element_type=F32)

    @pl.when(j < n_lead)
    def _():
        o_ref[...] = jnp.dot(hn_ref[...], w_ref[...], preferred_element_type=F32).astype(BF16)

    @pl.when(j >= n_lead)
    def _():
        o_ref[...] = jnp.dot(hn_ref[...], wg_ref[...], preferred_element_type=F32).astype(BF16)


def _in_proj(x2, norm_w, w_all, w_gates, w_dt):
    t = x2.shape[0]
    tm = min(1024, t)
    tn = 1024
    per = CHUNK_W // tn
    n_lead = DT_COL0 // tn
    return pl.pallas_call(
        functools.partial(_in_proj_kernel, n_lead=n_lead),
        grid=(t // tm, N_CHUNKS * per),
        in_specs=[
            pl.BlockSpec((tm, D_MODEL), lambda i, j: (i, 0)),
            pl.BlockSpec((1, D_MODEL), lambda i, j: (0, 0)),
            pl.BlockSpec((D_MODEL, tn), lambda i, j: (0, jnp.minimum(j, n_lead - 1))),
            pl.BlockSpec((D_MODEL, tn), lambda i, j: (0, jnp.maximum(j - n_lead, 0))),
            pl.BlockSpec((D_MODEL, LANES), lambda i, j: (0, 0)),
        ],
        out_specs=[
            pl.BlockSpec((None, tm, tn), lambda i, j: (j // per, i, j % per)),
            pl.BlockSpec((tm, LANES), lambda i, j: (i, 0)),
        ],
        out_shape=[
            jax.ShapeDtypeStruct((N_CHUNKS, t, CHUNK_W), BF16),
            jax.ShapeDtypeStruct((t, LANES), F32),
        ],
        scratch_shapes=[pltpu.VMEM((tm, D_MODEL), BF16)],
        compiler_params=_compiler_params(2, 48),
        name="in_proj",
    )(x2, norm_w, w_all, w_gates, w_dt)


ATTN_ROWS = PATTERNS[-1][1] * ATTN_BLOCK
ATTN_HEADS_PER_STEP = 2
ATTN_STEP_W = ATTN_HEADS_PER_STEP * HEAD_DIM
ATTN_UNROLL = 16


def _attn_kernel(q_ref, k_ref, v_ref, z_ref, dist_ref, dist0_ref, slope_ref, o_ref,
                 slabs, kd, vd, qd, oacc, lacc, bias):
    i = pl.program_id(2)
    blk = ATTN_BLOCK
    rows = ATTN_ROWS
    nh = ATTN_HEADS_PER_STEP
    scale = HEAD_DIM ** -0.5

    geo = []
    kv_base = q_base = 0
    for _, dil in PATTERNS:
        sub = rows // dil
        geo.append((dil, sub, kv_base, sub + blk, q_base))
        kv_base += dil * (sub + blk)
        q_base += rows if dil > 1 else 0

    prev_rows = [(base + r * pitch, sub) for dil, sub, base, pitch, _ in geo for r in range(dil)]

    @pl.when(i == 0)
    def _():
        for p0, _ in prev_rows:
            kd[p0:p0 + blk, :] = jnp.zeros((blk, ATTN_STEP_W), BF16)
            vd[p0:p0 + blk, :] = jnp.zeros((blk, ATTN_STEP_W), BF16)

    @pl.when(i != 0)
    def _():
        for p0, sub in prev_rows:
            kd[p0:p0 + blk, :] = kd[p0 + sub:p0 + sub + blk, :]
            vd[p0:p0 + blk, :] = vd[p0 + sub:p0 + sub + blk, :]

    def regroup(src_ref, dst, is_kv):
        def stage(rb, carry):
            rws = pl.ds(pl.multiple_of(rb * blk, blk), blk)
            for s in range(nh):
                slabs[0, s, rws, :] = src_ref[rws, s * LANES:(s + 1) * LANES].astype(F32)
            return carry

        lax.fori_loop(0, rows // blk, stage, 0, unroll=4)
        level, d_prev, sub_prev = 0, 1, rows
        for li, (dil, sub, base, pitch, qb) in enumerate(geo):
            nblk = sub // blk
            if dil == 1:
                if is_kv:
                    def copy(rb, carry, base=base):
                        r0 = pl.multiple_of(rb * blk, blk)
                        dst[pl.ds(base + blk + r0, blk), :] = src_ref[pl.ds(r0, blk), :]
                        return carry

                    lax.fori_loop(0, nblk, copy, 0, unroll=4)
                continue
            ratio = dil // d_prev
            keep = li < len(geo) - 1

            def gather(it, carry, sub=sub, base=base, pitch=pitch, qb=qb, nblk=nblk,
                       ratio=ratio, d_prev=d_prev, sub_prev=sub_prev, level=level, keep=keep):
                r = it // nblk
                b = it % nblk
                src0 = (r % d_prev) * sub_prev + r // d_prev + b * blk * ratio
                p0 = base + r * pitch + blk if is_kv else qb + r * sub
                p0 = pl.multiple_of(p0 + b * blk, blk)
                for s in range(nh):
                    val = slabs[level, s, pl.ds(src0, blk, stride=ratio), :]
                    dst[pl.ds(p0, blk), s * LANES:(s + 1) * LANES] = val.astype(BF16)
                    if keep:
                        slabs[1 - level, s, pl.ds(pl.multiple_of(r * sub + b * blk, blk), blk), :] = val
                return carry

            lax.fori_loop(0, dil * nblk, gather, 0, unroll=4)
            level, d_prev, sub_prev = 1 - level, dil, sub

    regroup(k_ref, kd, True)
    regroup(v_ref, vd, True)
    regroup(q_ref, qd, False)

    for pi, (dil, _, _, _, _) in enumerate(geo):
        for s in range(nh):
            slope = slope_ref[s] * (float(dil) * LOG2E)
            bias[(pi * nh + s) * 2] = slope * dist_ref[...]
            bias[(pi * nh + s) * 2 + 1] = slope * dist0_ref[...]

    ones = jnp.ones((2 * blk, LANES), BF16)

    def unit(q, kc, vc, bias_t):
        s = lax.dot_general(q, kc, (((1,), (1,)), ((), ())), preferred_element_type=F32)
        s = s * (scale * LOG2E) - bias_t
        m = jnp.max(s, axis=-1, keepdims=True)
        p = jnp.exp2(s - m)
        pv = jnp.dot(p.astype(BF16), jnp.concatenate([vc, ones], axis=1),
                     preferred_element_type=F32)
        l_b = pv[:, LANES:]
        m_b = jnp.broadcast_to(m, (blk, LANES))
        return pv[:, :LANES] * (1.0 / l_b), m_b + jnp.log(l_b) * LOG2E

    for pi, (dil, sub, base, pitch, qb) in enumerate(geo):
        nblk = sub // blk

        def body(it, carry, dil=dil, sub=sub, base=base, pitch=pitch, qb=qb, nblk=nblk, pi=pi):
            r = it // nblk
            b = it % nblk
            no_prev = jnp.logical_and(i == 0, b == 0).astype(jnp.int32)
            q0 = pl.multiple_of(qb + r * sub + b * blk, blk)
            k0 = pl.multiple_of(base + r * pitch + b * blk, blk)
            t0 = r + b * blk * dil
            if dil == 1:
                tok = pl.ds(pl.multiple_of(t0, blk), blk)
            else:
                tok = pl.ds(t0, blk, stride=dil)
            for s in range(nh):
                c0 = s * LANES
                q_src = q_ref if dil == 1 else qd
                o, lse = unit(q_src[pl.ds(q0, blk), c0:c0 + LANES],
                              kd[pl.ds(k0, 2 * blk), c0:c0 + LANES],
                              vd[pl.ds(k0, 2 * blk), c0:c0 + LANES],
                              bias[(pi * nh + s) * 2 + no_prev])
                oacc[pi * nh + s, tok, :] = o
                lacc[pi * nh + s, tok, :] = lse
            return carry

        lax.fori_loop(0, dil * nblk, body, 0, unroll=ATTN_UNROLL)

    n_pat = len(geo)

    def mix(bi, carry):
        tok = pl.ds(pl.multiple_of(bi * blk, blk), blk)
        for s in range(nh):
            lses = [lacc[pi * nh + s, tok, :] for pi in range(n_pat)]
            top = functools.reduce(jnp.maximum, lses)
            ws = [jnp.exp2(l - top) for l in lses]
            num = sum(w * oacc[pi * nh + s, tok, :] for pi, w in enumerate(ws))
            z = z_ref[tok, s * LANES:(s + 1) * LANES].astype(F32)
            o_ref[tok, s * LANES:(s + 1) * LANES] = (
                num * (1.0 / sum(ws)) * _silu(z)).astype(BF16)
        return carry

    lax.fori_loop(0, rows // blk, mix, 0, unroll=4)


def _dist_tables():
    blk = ATTN_BLOCK
    qi = jnp.arange(blk)[:, None]
    ki = jnp.arange(2 * blk)[None, :]
    dist = qi - ki + blk
    valid = (dist >= 0) & (dist <= blk)
    d = jnp.where(valid, dist.astype(F32), jnp.inf)
    d0 = jnp.where(ki >= blk, d, jnp.inf)
    return d, d0


def _attention(proj, batch, seq):
    assert all(w // d == ATTN_BLOCK for w, d in PATTERNS)
    t = batch * seq
    rows = ATTN_ROWS
    steps = seq // rows
    n_groups = ATTN_HEADS // ATTN_HEADS_PER_STEP
    slopes = jnp.asarray([2.0 ** (-8.0 * (h + 1) / ATTN_HEADS) for h in range(ATTN_HEADS)], F32)
    slope_tab = jnp.broadcast_to(slopes[:, None, None], (ATTN_HEADS, 1, 2 * ATTN_BLOCK))
    dist, dist0 = _dist_tables()
    kv_rows = sum(d * (rows // d + ATTN_BLOCK) for _, d in PATTERNS)
    q_rows = sum(rows for _, d in PATTERNS if d > 1)

    def chunk_spec(ch):
        return pl.BlockSpec((None, rows, ATTN_STEP_W), lambda b, g, i: (ch, b * steps + i, g))

    tab_spec = pl.BlockSpec((ATTN_BLOCK, 2 * ATTN_BLOCK), lambda b, g, i: (0, 0))
    slope_spec = pl.BlockSpec((ATTN_HEADS_PER_STEP, 1, 2 * ATTN_BLOCK), lambda b, g, i: (g, 0, 0))
    return pl.pallas_call(
        _attn_kernel,
        grid=(batch, n_groups, steps),
        in_specs=[chunk_spec(CH_Q), chunk_spec(CH_K), chunk_spec(CH_V), chunk_spec(CH_ZA),
                  tab_spec, tab_spec, slope_spec],
        out_specs=pl.BlockSpec((rows, ATTN_STEP_W), lambda b, g, i: (b * steps + i, g)),
        out_shape=jax.ShapeDtypeStruct((t, ATTN_WIDTH), BF16),
        scratch_shapes=[
            pltpu.VMEM((2, ATTN_HEADS_PER_STEP, rows, LANES), F32),
            pltpu.VMEM((kv_rows, ATTN_STEP_W), BF16),
            pltpu.VMEM((kv_rows, ATTN_STEP_W), BF16),
            pltpu.VMEM((q_rows, ATTN_STEP_W), BF16),
            pltpu.VMEM((len(PATTERNS) * ATTN_HEADS_PER_STEP, rows, LANES), F32),
            pltpu.VMEM((len(PATTERNS) * ATTN_HEADS_PER_STEP, rows, LANES), F32),
            pltpu.VMEM((len(PATTERNS) * ATTN_HEADS_PER_STEP * 2, ATTN_BLOCK, 2 * ATTN_BLOCK), F32),
        ],
        compiler_params=_compiler_params(3, 48),
        name="attn",
    )(proj, proj, proj, proj, dist, dist0, slope_tab)


def _ssd_kernel(xs_ref, bc_ref, zs_ref, dt_ref, convw_ref, convb_ref, dtb_ref, alog_ref,
                dskip_ref, normw_ref, tril_ref, ws_ref, y_ref,
                xext, state, xs_s, b_s, c_s, acum_s, acumt_s, dt_s, xsc_s, y_s, ynbuf,
                *, ts, steps):
    L = SSM_CHUNK
    n_state = SSM_STATE
    n_slabs = SSM_CONV_DIM // LANES
    g_step = pl.program_id(0)
    cur = g_step % 2

    @pl.when(g_step == 0)
    def _():
        ynbuf[1] = jnp.zeros((ts, SSM_INNER), BF16)

    @pl.when(g_step % steps == 0)
    def _():
        state[...] = jnp.zeros_like(state)
        for s in range(n_slabs):
            xext[s, 0:8, :] = jnp.zeros((8, LANES), F32)

    row = lax.broadcasted_iota(jnp.int32, (L, L), 0)
    col = lax.broadcasted_iota(jnp.int32, (L, L), 1)
    causal = row >= col
    low_half = col < SSM_HEAD_DIM
    neg_a = -jnp.exp(alog_ref[...])

    def chunk_body(c, carry):
        r0 = pl.multiple_of(c * L, L)

        per_chunk = CHUNK_W // LANES
        for s in range(n_slabs):
            lo = s * LANES
            c0 = (s % per_chunk) * LANES
            if s < 2 * per_chunk:
                raw = xs_ref[s // per_chunk, pl.ds(r0, L), c0:c0 + LANES]
            else:
                raw = bc_ref[pl.ds(r0, L), c0:c0 + LANES]
            xext[s, 8:8 + L, :] = raw.astype(F32)
            acc = jnp.broadcast_to(convb_ref[:, lo:lo + LANES], (L, LANES))
            for k in range(SSM_CONV):
                shift = 8 - (SSM_CONV - 1) + k
                acc = acc + convw_ref[k:k + 1, lo:lo + LANES] * xext[s, shift:shift + L, :]
            xext[s, 0:8, :] = xext[s, L:L + 8, :]
            xc = _silu(acc)
            if lo < SSM_INNER:
                xs_s[:, lo:lo + LANES] = xc
            elif lo < SSM_INNER + SSM_GROUPS * n_state:
                b_s[:, lo - SSM_INNER:lo - SSM_INNER + LANES] = xc.astype(BF16)
            else:
                o2 = lo - SSM_INNER - SSM_GROUPS * n_state
                c_s[:, o2:o2 + LANES] = xc

        dt_in = dt_ref[pl.ds(r0, L), :] + dtb_ref[...]
        dtv = jnp.maximum(dt_in, 0.0) + jnp.log(1.0 + jnp.exp(-jnp.abs(dt_in)))
        da = dtv * neg_a
        acum = jnp.dot(tril_ref[...], da, preferred_element_type=F32,
                       precision=lax.Precision.HIGHEST)
        acum_s[...] = acum
        acumt_s[...] = acum.T
        dt_s[...] = dtv

        proj_w = D_MODEL // SSM_GROUPS
        for g in range(SSM_GROUPS):
            p0 = g * proj_w
            y_ref[pl.ds(r0, L), p0:p0 + proj_w] = jnp.dot(
                ynbuf[1 - cur, pl.ds(r0, L), :], ws_ref[:, p0:p0 + proj_w],
                preferred_element_type=F32).astype(BF16)

            n0 = g * n_state
            bg = b_s[:, n0:n0 + n_state]
            cg = c_s[:, n0:n0 + n_state]
            cb = lax.dot_general(cg.astype(BF16), bg, (((1,), (1,)), ((), ())),
                                 preferred_element_type=F32)
            cb = jnp.where(causal, cb, 0.0)
            elast_tiles = []
            for jp in range(HEADS_PER_GROUP // 2):
                col0 = (g * (HEADS_PER_GROUP // 2) + jp) * LANES
                lhs = []
                cols = []
                for e in range(2):
                    h = g * HEADS_PER_GROUP + jp * 2 + e
                    col_a = acum_s[:, h:h + 1]
                    row_a = acumt_s[h:h + 1, :]
                    last = acum_s[L - 1:L, h:h + 1]
                    seg = col_a - row_a
                    decay = jnp.exp(jnp.minimum(seg, 0.0))
                    m_mat = (cb * decay).astype(BF16)
                    c_scaled = (cg * jnp.exp(col_a)).astype(BF16)
                    lhs.append(jnp.concatenate([m_mat, c_scaled], axis=1))
                    cols.append((dt_s[:, h:h + 1], jnp.exp(last - col_a), jnp.exp(last)))
                dt_pair = jnp.where(low_half, cols[0][0], cols[1][0])
                w_pair = jnp.where(low_half, cols[0][1], cols[1][1])
                elast_tiles.append(jnp.where(low_half[0:1, :], cols[0][2], cols[1][2]))
                xs_pair = xs_s[:, col0:col0 + LANES]
                xdt = xs_pair * dt_pair
                xsc_s[:, col0:col0 + LANES] = (xdt * w_pair).astype(BF16)
                rhs = jnp.concatenate([xdt.astype(BF16), state[:, col0:col0 + LANES].astype(BF16)],
                                      axis=0)
                y0 = jnp.dot(lhs[0], rhs, preferred_element_type=F32)
                y1 = jnp.dot(lhs[1], rhs, preferred_element_type=F32)
                y_s[:, col0:col0 + LANES] = (jnp.where(low_half, y0, y1)
                                             + dskip_ref[:, col0:col0 + LANES] * xs_pair)
            g0 = g * HEADS_PER_GROUP * SSM_HEAD_DIM
            gw = HEADS_PER_GROUP * SSM_HEAD_DIM
            upd = lax.dot_general(bg, xsc_s[:, g0:g0 + gw], (((0,), (0,)), ((), ())),
                                  preferred_element_type=F32)
            elast = jnp.concatenate(elast_tiles, axis=1)
            state[:, g0:g0 + gw] = state[:, g0:g0 + gw] * elast + upd

        ssq = jnp.zeros((L, 1), F32)
        for hf in range(2):
            z = zs_ref[hf, pl.ds(r0, L), :].astype(F32)
            gy = y_s[:, hf * CHUNK_W:(hf + 1) * CHUNK_W] * _silu(z)
            y_s[:, hf * CHUNK_W:(hf + 1) * CHUNK_W] = gy
            ssq = ssq + jnp.sum(gy * gy, axis=-1, keepdims=True)
        inv = lax.rsqrt(ssq * (1.0 / SSM_INNER) + RMS_EPS)
        ynbuf[cur, pl.ds(r0, L), :] = (y_s[...] * inv * normw_ref[...]).astype(BF16)
        return carry

    lax.fori_loop(0, ts // L, chunk_body, 0)


def _ssd(proj, dt_raw, conv_w, conv_b, dt_bias, a_log, d_skip, ssm_norm_w, w_s, batch, seq):
    t = batch * seq
    ts = min(256, seq)
    steps = seq // ts
    n_steps = batch * steps
    last = n_steps - 1
    pad = LANES - SSM_HEADS
    dtb = jnp.pad(dt_bias.astype(F32), (0, pad)).reshape(1, LANES)
    alog = jnp.pad(a_log.astype(F32), (0, pad)).reshape(1, LANES)
    dskip = jnp.repeat(d_skip.astype(F32), SSM_HEAD_DIM).reshape(1, SSM_INNER)
    tril = jnp.tril(jnp.ones((SSM_CHUNK, SSM_CHUNK), F32))

    def full(shape, **kw):
        return pl.BlockSpec(shape, lambda g: (0,) * len(shape), **kw)

    def rows_of(g):
        return jnp.minimum(g, last)

    return pl.pallas_call(
        functools.partial(_ssd_kernel, ts=ts, steps=steps),
        grid=(n_steps + 1,),
        in_specs=[
            pl.BlockSpec((2, ts, CHUNK_W), lambda g: (CH_XS // 2, rows_of(g), 0)),
            pl.BlockSpec((None, ts, CHUNK_W), lambda g: (CH_BC, rows_of(g), 0)),
            pl.BlockSpec((2, ts, CHUNK_W), lambda g: (CH_ZS // 2, rows_of(g), 0)),
            pl.BlockSpec((ts, LANES), lambda g: (rows_of(g), 0)),
            full((SSM_CONV, SSM_CONV_DIM)),
            full((1, SSM_CONV_DIM)),
            full((1, LANES)),
            full((1, LANES)),
            full((1, SSM_INNER)),
            full((1, SSM_INNER)),
            full((SSM_CHUNK, SSM_CHUNK)),
            full((SSM_INNER, D_MODEL), pipeline_mode=pl.Buffered(1)),
        ],
        out_specs=pl.BlockSpec((ts, D_MODEL), lambda g: (jnp.maximum(g - 1, 0), 0)),
        out_shape=jax.ShapeDtypeStruct((t, D_MODEL), BF16),
        scratch_shapes=[
            pltpu.VMEM((SSM_CONV_DIM // LANES, SSM_CHUNK + 8, LANES), F32),
            pltpu.VMEM((SSM_STATE, SSM_INNER), F32),
            pltpu.VMEM((SSM_CHUNK, SSM_INNER), F32),
            pltpu.VMEM((SSM_CHUNK, SSM_GROUPS * SSM_STATE), BF16),
            pltpu.VMEM((SSM_CHUNK, SSM_GROUPS * SSM_STATE), F32),
            pltpu.VMEM((SSM_CHUNK, LANES), F32),
            pltpu.VMEM((LANES, SSM_CHUNK), F32),
            pltpu.VMEM((SSM_CHUNK, LANES), F32),
            pltpu.VMEM((SSM_CHUNK, SSM_INNER), BF16),
            pltpu.VMEM((SSM_CHUNK, SSM_INNER), F32),
            pltpu.VMEM((2, ts, SSM_INNER), BF16),
        ],
        compiler_params=_compiler_params(1, 56),
        name="ssd",
    )(proj, proj, proj, dt_raw, conv_w.astype(F32), conv_b.astype(F32).reshape(1, SSM_CONV_DIM),
      dtb, alog, dskip, ssm_norm_w.astype(F32).reshape(1, SSM_INNER), tril, w_s)


def _tail_kernel(oa_ref, ys_ref, ga_ref, gs_ref, x_ref, wa_ref, wo_ref, nw_ref, o_ref):
    a = jnp.dot(oa_ref[...], wa_ref[...], preferred_element_type=F32)
    ga = _sigmoid(ga_ref[...].astype(F32))
    gs = _sigmoid(gs_ref[...].astype(F32))
    merged = (ga * a + gs * ys_ref[...].astype(F32)).astype(BF16)
    r = x_ref[...] + jnp.dot(merged, wo_ref[...], preferred_element_type=F32)
    ms = jnp.mean(r * r, axis=-1, keepdims=True)
    o_ref[...] = r * lax.rsqrt(ms + RMS_EPS) * nw_ref[...]


def _tail(o_a, y_s, proj, x2, w_a, w_out, final_norm_w):
    t = x2.shape[0]
    tm = min(256, t)

    def rows(shape):
        return pl.BlockSpec(shape, lambda i: (i, 0))

    def resident(shape):
        return pl.BlockSpec(shape, lambda i: (0, 0), pipeline_mode=pl.Buffered(1))

    return pl.pallas_call(
        _tail_kernel,
        grid=(t // tm,),
        in_specs=[
            rows((tm, ATTN_WIDTH)),
            rows((tm, D_MODEL)),
            pl.BlockSpec((None, tm, CHUNK_W), lambda i: (CH_GA, i, 0)),
            pl.BlockSpec((None, tm, CHUNK_W), lambda i: (CH_GS, i, 0)),
            rows((tm, D_MODEL)),
            resident((ATTN_WIDTH, D_MODEL)),
            resident((D_MODEL, D_MODEL)),
            resident((1, D_MODEL)),
        ],
        out_specs=rows((tm, D_MODEL)),
        out_shape=jax.ShapeDtypeStruct((t, D_MODEL), F32),
        compiler_params=_compiler_params(1, 48),
        name="tail",
    )(o_a, y_s, proj, proj, x2, w_a, w_out, final_norm_w)


def _layer(x, norm_w, w_in, conv_w, conv_b, dt_bias, a_log, d_skip, ssm_norm_w,
           w_attn_branch, w_ssm_branch, w_out, out_norm_w):
    batch, seq, _ = x.shape
    t = batch * seq
    x2 = x.reshape(t, D_MODEL)

    w_all = w_in.astype(BF16)
    w_gates = w_all[:, DT_COL0 + SSM_HEADS:]
    w_dt = jnp.pad(w_all[:, DT_COL0:DT_COL0 + SSM_HEADS], ((0, 0), (0, LANES - SSM_HEADS)))
    proj, dt_raw = _in_proj(x2, norm_w.astype(F32).reshape(1, D_MODEL), w_all, w_gates, w_dt)

    o_a = _attention(proj, batch, seq)
    y_s = _ssd(proj, dt_raw, conv_w, conv_b, dt_bias, a_log, d_skip, ssm_norm_w,
               w_ssm_branch.astype(BF16), batch, seq)
    out = _tail(o_a, y_s, proj, x2, w_attn_branch.astype(BF16), w_out.astype(BF16),
                out_norm_w.astype(F32).reshape(1, D_MODEL))
    return out.reshape(batch, seq, D_MODEL)


def kernel(x, norm_w, w_in, conv_w, conv_b, dt_bias, a_log, d_skip, ssm_norm_w,
           w_attn_branch, w_ssm_branch, w_out, final_norm_w):
    depth = norm_w.shape[0]
    assert depth == 1, "the residual stream between layers is not normalised; only depth 1 is fused"
    assert x.shape[1] % ATTN_ROWS == 0 and x.shape[2] == D_MODEL
    return _layer(x, norm_w[0], w_in[0], conv_w[0], conv_b[0], dt_bias[0], a_log[0], d_skip[0],
                  ssm_norm_w[0], w_attn_branch[0], w_ssm_branch[0], w_out[0], final_norm_w)
```

```python
import functools

import jax
import jax.numpy as jnp
from jax import lax
from jax.experimental import pallas as pl
from jax.experimental.pallas import tpu as pltpu

D_MODEL = 2048
ATTN_HEADS = 16
HEAD_DIM = 128
ATTN_WIDTH = ATTN_HEADS * HEAD_DIM
PATTERNS = ((128, 1), (512, 4), (2048, 16))
ATTN_BLOCK = 128

SSM_INNER = 4096
SSM_HEAD_DIM = 64
SSM_HEADS = 64
SSM_GROUPS = 8
SSM_STATE = 128
SSM_CONV = 4
SSM_CHUNK = 128
SSM_CONV_DIM = SSM_INNER + 2 * SSM_GROUPS * SSM_STATE
HEADS_PER_GROUP = SSM_HEADS // SSM_GROUPS
RMS_EPS = 1e-6
LOG2E = 1.4426950408889634

LANES = 128
CHUNK_W = D_MODEL
CH_Q, CH_K, CH_V, CH_ZA, CH_ZS, CH_XS, CH_BC, CH_GA, CH_GS = 0, 1, 2, 3, 4, 6, 8, 9, 10
N_CHUNKS = 11
DT_COL0 = 4 * ATTN_WIDTH + SSM_INNER + SSM_CONV_DIM

F32 = jnp.float32
BF16 = jnp.bfloat16
MIB = 1024 * 1024


def _sigmoid(v):
    return 0.5 + 0.5 * jnp.tanh(0.5 * v)


def _silu(v):
    h = 0.5 * v
    return h + h * jnp.tanh(h)


def _compiler_params(n_grid, vmem_mib):
    return pltpu.CompilerParams(
        dimension_semantics=("arbitrary",) * n_grid,
        vmem_limit_bytes=vmem_mib * MIB,
    )


def _in_proj_kernel(x_ref, nw_ref, w_ref, wg_ref, wdt_ref, o_ref, dt_ref, hn_ref, *, n_lead):
    j = pl.program_id(1)

    @pl.when(j == 0)
    def _():
        x = x_ref[...]
        ms = jnp.mean(x * x, axis=-1, keepdims=True)
        hn = (x * lax.rsqrt(ms + RMS_EPS) * nw_ref[...]).astype(BF16)
        hn_ref[...] = hn
        dt_ref[...] = jnp.dot(hn, wdt_ref[...], preferred_element_type=F32)

    @pl.when(j < n_lead)
    def _():
        o_ref[...] = jnp.dot(hn_ref[...], w_ref[...], preferred_element_type=F32).astype(BF16)

    @pl.when(j >= n_lead)
    def _():
        o_ref[...] = jnp.dot(hn_ref[...], wg_ref[...], preferred_element_type=F32).astype(BF16)


def _in_proj(x2, norm_w, w_all, w_gates, w_dt):
    t = x2.shape[0]
    tm = min(1024, t)
    tn = 1024
    per = CHUNK_W // tn
    n_lead = DT_COL0 // tn
    return pl.pallas_call(
        functools.partial(_in_proj_kernel, n_lead=n_lead),
        grid=(t // tm, N_CHUNKS * per),
        in_specs=[
            pl.BlockSpec((tm, D_MODEL), lambda i, j: (i, 0)),
            pl.BlockSpec((1, D_MODEL), lambda i, j: (0, 0)),
            pl.BlockSpec((D_MODEL, tn), lambda i, j: (0, jnp.minimum(j, n_lead - 1))),
            pl.BlockSpec((D_MODEL, tn), lambda i, j: (0, jnp.maximum(j - n_lead, 0))),
            pl.BlockSpec((D_MODEL, LANES), lambda i, j: (0, 0)),
        ],
        out_specs=[
            pl.BlockSpec((None, tm, tn), lambda i, j: (j // per, i, j % per)),
            pl.BlockSpec((tm, LANES), lambda i, j: (i, 0)),
        ],
        out_shape=[
            jax.ShapeDtypeStruct((N_CHUNKS, t, CHUNK_W), BF16),
            jax.ShapeDtypeStruct((t, LANES), F32),
        ],
        scratch_shapes=[pltpu.VMEM((tm, D_MODEL), BF16)],
        compiler_params=_compiler_params(2, 48),
        name="in_proj",
    )(x2, norm_w, w_all, w_gates, w_dt)


ATTN_ROWS = PATTERNS[-1][1] * ATTN_BLOCK
ATTN_HEADS_PER_STEP = 2
ATTN_STEP_W = ATTN_HEADS_PER_STEP * HEAD_DIM
ATTN_UNROLL = 16


def _attn_kernel(q_ref, k_ref, v_ref, z_ref, dist_ref, dist0_ref, slope_ref, o_ref,
                 slabs, kd, vd, qd, oacc, lacc, bias):
    i = pl.program_id(2)
    blk = ATTN_BLOCK
    rows = ATTN_ROWS
    nh = ATTN_HEADS_PER_STEP
    scale = HEAD_DIM ** -0.5

    geo = []
    kv_base = q_base = 0
    for _, dil in PATTERNS:
        sub = rows // dil
        geo.append((dil, sub, kv_base, sub + blk, q_base))
        kv_base += dil * (sub + blk)
        q_base += rows if dil > 1 else 0

    prev_rows = [(base + r * pitch, sub) for dil, sub, base, pitch, _ in geo for r in range(dil)]

    @pl.when(i == 0)
    def _():
        for p0, _ in prev_rows:
            kd[p0:p0 + blk, :] = jnp.zeros((blk, ATTN_STEP_W), BF16)
            vd[p0:p0 + blk, :] = jnp.zeros((blk, ATTN_STEP_W), BF16)

    @pl.when(i != 0)
    def _():
        for p0, sub in prev_rows:
            kd[p0:p0 + blk, :] = kd[p0 + sub:p0 + sub + blk, :]
            vd[p0:p0 + blk, :] = vd[p0 + sub:p0 + sub + blk, :]

    def regroup(src_ref, dst, is_kv):
        def stage(rb, carry):
            rws = pl.ds(pl.multiple_of(rb * blk, blk), blk)
            for s in range(nh):
                slabs[0, s, rws, :] = src_ref[rws, s * LANES:(s + 1) * LANES].astype(F32)
            return carry

        lax.fori_loop(0, rows // blk, stage, 0, unroll=True)
        level, d_prev, sub_prev = 0, 1, rows
        for li, (dil, sub, base, pitch, qb) in enumerate(geo):
            nblk = sub // blk
            if dil == 1:
                if is_kv:
                    def copy(rb, carry, base=base):
                        r0 = pl.multiple_of(rb * blk, blk)
                        dst[pl.ds(base + blk + r0, blk), :] = src_ref[pl.ds(r0, blk), :]
                        return carry

                    lax.fori_loop(0, nblk, copy, 0, unroll=True)
                continue
            ratio = dil // d_prev
            keep = li < len(geo) - 1

            def gather(it, carry, sub=sub, base=base, pitch=pitch, qb=qb, nblk=nblk,
                       ratio=ratio, d_prev=d_prev, sub_prev=sub_prev, level=level, keep=keep):
                r = it // nblk
                b = it % nblk
                src0 = (r % d_prev) * sub_prev + r // d_prev + b * blk * ratio
                p0 = base + r * pitch + blk if is_kv else qb + r * sub
                p0 = pl.multiple_of(p0 + b * blk, blk)
                for s in range(nh):
                    val = slabs[level, s, pl.ds(src0, blk, stride=ratio), :]
                    dst[pl.ds(p0, blk), s * LANES:(s + 1) * LANES] = val.astype(BF16)
                    if keep:
                        slabs[1 - level, s, pl.ds(pl.multiple_of(r * sub + b * blk, blk), blk), :] = val
                return carry

            lax.fori_loop(0, dil * nblk, gather, 0, unroll=True)
            level, d_prev, sub_prev = 1 - level, dil, sub

    regroup(k_ref, kd, True)
    regroup(v_ref, vd, True)
    regroup(q_ref, qd, False)

    for pi, (dil, _, _, _, _) in enumerate(geo):
        for s in range(nh):
            slope = slope_ref[s] * (float(dil) * LOG2E)
            bias[(pi * nh + s) * 2] = slope * dist_ref[...]
            bias[(pi * nh + s) * 2 + 1] = slope * dist0_ref[...]

    ones = jnp.ones((2 * blk, LANES), BF16)

    def unit(q, kc, vc, bias_t):
        s = lax.dot_general(q, kc, (((1,), (1,)), ((), ())), preferred_element_type=F32)
        s = s * (scale * LOG2E) - bias_t
        m = jnp.max(s, axis=-1, keepdims=True)
        p = jnp.exp2(s - m)
        pv = jnp.dot(p.astype(BF16), jnp.concatenate([vc, ones], axis=1),
                     preferred_element_type=F32)
        l_b = pv[:, LANES:]
        m_b = jnp.broadcast_to(m, (blk, LANES))
        return pv[:, :LANES] * (1.0 / l_b), m_b + jnp.log(l_b) * LOG2E

    for pi, (dil, sub, base, pitch, qb) in enumerate(geo):
        nblk = sub // blk

        def body(it, carry, dil=dil, sub=sub, base=base, pitch=pitch, qb=qb, nblk=nblk, pi=pi):
            r = it // nblk
            b = it % nblk
            no_prev = jnp.logical_and(i == 0, b == 0).astype(jnp.int32)
            q0 = pl.multiple_of(qb + r * sub + b * blk, blk)
            k0 = pl.multiple_of(base + r * pitch + b * blk, blk)
            t0 = r + b * blk * dil
            if dil == 1:
                tok = pl.ds(pl.multiple_of(t0, blk), blk)
            else:
                tok = pl.ds(t0, blk, stride=dil)
            for s in range(nh):
                c0 = s * LANES
                q_src = q_ref if dil == 1 else qd
                o, lse = unit(q_src[pl.ds(q0, blk), c0:c0 + LANES],
                              kd[pl.ds(k0, 2 * blk), c0:c0 + LANES],
                              vd[pl.ds(k0, 2 * blk), c0:c0 + LANES],
                              bias[(pi * nh + s) * 2 + no_prev])
                oacc[pi * nh + s, tok, :] = o
                lacc[pi * nh + s, tok, :] = lse
            return carry

        lax.fori_loop(0, dil * nblk, body, 0, unroll=ATTN_UNROLL)

    n_pat = len(geo)

    def mix(bi, carry):
        tok = pl.ds(pl.multiple_of(bi * blk, blk), blk)
        for s in range(nh):
            lses = [lacc[pi * nh + s, tok, :] for pi in range(n_pat)]
            top = functools.reduce(jnp.maximum, lses)
            ws = [jnp.exp2(l - top) for l in lses]
            num = sum(w * oacc[pi * nh + s, tok, :] for pi, w in enumerate(ws))
            z = z_ref[tok, s * LANES:(s + 1) * LANES].astype(F32)
            o_ref[tok, s * LANES:(s + 1) * LANES] = (
                num * (1.0 / sum(ws)) * _silu(z)).astype(BF16)
        return carry

    lax.fori_loop(0, rows // blk, mix, 0, unroll=4)


def _dist_tables():
    blk = ATTN_BLOCK
    qi = jnp.arange(blk)[:, None]
    ki = jnp.arange(2 * blk)[None, :]
    dist = qi - ki + blk
    valid = (dist >= 0) & (dist <= blk)
    d = jnp.where(valid, dist.astype(F32), jnp.inf)
    d0 = jnp.where(ki >= blk, d, jnp.inf)
    return d, d0


def _attention(proj, batch, seq):
    assert all(w // d == ATTN_BLOCK for w, d in PATTERNS)
    t = batch * seq
    rows = ATTN_ROWS
    steps = seq // rows
    n_groups = ATTN_HEADS // ATTN_HEADS_PER_STEP
    slopes = jnp.asarray([2.0 ** (-8.0 * (h + 1) / ATTN_HEADS) for h in range(ATTN_HEADS)], F32)
    slope_tab = jnp.broadcast_to(slopes[:, None, None], (ATTN_HEADS, 1, 2 * ATTN_BLOCK))
    dist, dist0 = _dist_tables()
    kv_rows = sum(d * (rows // d + ATTN_BLOCK) for _, d in PATTERNS)
    q_rows = sum(rows for _, d in PATTERNS if d > 1)

    def chunk_spec(ch):
        return pl.BlockSpec((None, rows, ATTN_STEP_W), lambda b, g, i: (ch, b * steps + i, g))

    tab_spec = pl.BlockSpec((ATTN_BLOCK, 2 * ATTN_BLOCK), lambda b, g, i: (0, 0))
    slope_spec = pl.BlockSpec((ATTN_HEADS_PER_STEP, 1, 2 * ATTN_BLOCK), lambda b, g, i: (g, 0, 0))
    return pl.pallas_call(
        _attn_kernel,
        grid=(batch, n_groups, steps),
        in_specs=[chunk_spec(CH_Q), chunk_spec(CH_K), chunk_spec(CH_V), chunk_spec(CH_ZA),
                  tab_spec, tab_spec, slope_spec],
        out_specs=pl.BlockSpec((rows, ATTN_STEP_W), lambda b, g, i: (b * steps + i, g)),
        out_shape=jax.ShapeDtypeStruct((t, ATTN_WIDTH), BF16),
        scratch_shapes=[
            pltpu.VMEM((2, ATTN_HEADS_PER_STEP, rows, LANES), F32),
            pltpu.VMEM((kv_rows, ATTN_STEP_W), BF16),
            pltpu.VMEM((kv_rows, ATTN_STEP_W), BF16),
            pltpu.VMEM((q_rows, ATTN_STEP_W), BF16),
            pltpu.VMEM((len(PATTERNS) * ATTN_HEADS_PER_STEP, rows, LANES), F32),
            pltpu.VMEM((len(PATTERNS) * ATTN_HEADS_PER_STEP, rows, LANES), F32),
            pltpu.VMEM((len(PATTERNS) * ATTN_HEADS_PER_STEP * 2, ATTN_BLOCK, 2 * ATTN_BLOCK), F32),
        ],
        compiler_params=_compiler_params(3, 48),
        name="attn",
    )(proj, proj, proj, proj, dist, dist0, slope_tab)


def _ssd_kernel(xs_ref, bc_ref, zs_ref, dt_ref, convw_ref, convb_ref, dtb_ref, alog_ref,
                dskip_ref, normw_ref, tril_ref, ws_ref, y_ref,
                xext, state, xs_s, b_s, c_s, acum_s, acumt_s, dt_s, xsc_s, y_s, ynbuf,
                *, ts, steps):
    L = SSM_CHUNK
    n_state = SSM_STATE
    n_slabs = SSM_CONV_DIM // LANES
    g_step = pl.program_id(0)
    cur = g_step % 2

    @pl.when(g_step == 0)
    def _():
        ynbuf[1] = jnp.zeros((ts, SSM_INNER), BF16)

    @pl.when(g_step % steps == 0)
    def _():
        state[...] = jnp.zeros_like(state)
        for s in range(n_slabs):
            xext[s, 0:8, :] = jnp.zeros((8, LANES), F32)

    row = lax.broadcasted_iota(jnp.int32, (L, L), 0)
    col = lax.broadcasted_iota(jnp.int32, (L, L), 1)
    causal = row >= col
    low_half = col < SSM_HEAD_DIM
    neg_a = -jnp.exp(alog_ref[...])

    def chunk_body(c, carry):
        r0 = pl.multiple_of(c * L, L)

        per_chunk = CHUNK_W // LANES
        for s in range(n_slabs):
            lo = s * LANES
            c0 = (s % per_chunk) * LANES
            if s < 2 * per_chunk:
                raw = xs_ref[s // per_chunk, pl.ds(r0, L), c0:c0 + LANES]
            else:
                raw = bc_ref[pl.ds(r0, L), c0:c0 + LANES]
            xext[s, 8:8 + L, :] = raw.astype(F32)
            acc = jnp.broadcast_to(convb_ref[:, lo:lo + LANES], (L, LANES))
            for k in range(SSM_CONV):
                shift = 8 - (SSM_CONV - 1) + k
                acc = acc + convw_ref[k:k + 1, lo:lo + LANES] * xext[s, shift:shift + L, :]
            xext[s, 0:8, :] = xext[s, L:L + 8, :]
            xc = _silu(acc)
            if lo < SSM_INNER:
                xs_s[:, lo:lo + LANES] = xc
            elif lo < SSM_INNER + SSM_GROUPS * n_state:
                b_s[:, lo - SSM_INNER:lo - SSM_INNER + LANES] = xc.astype(BF16)
            else:
                o2 = lo - SSM_INNER - SSM_GROUPS * n_state
                c_s[:, o2:o2 + LANES] = xc

        dt_in = dt_ref[pl.ds(r0, L), :] + dtb_ref[...]
        dtv = jnp.maximum(dt_in, 0.0) + jnp.log(1.0 + jnp.exp(-jnp.abs(dt_in)))
        da = dtv * neg_a
        acum = jnp.dot(tril_ref[...], da, preferred_element_type=F32,
                       precision=lax.Precision.HIGHEST)
        acum_s[...] = acum
        acumt_s[...] = acum.T
        dt_s[...] = dtv

        proj_w = D_MODEL // SSM_GROUPS
        for g in range(SSM_GROUPS):
            p0 = g * proj_w
            y_ref[pl.ds(r0, L), p0:p0 + proj_w] = jnp.dot(
                ynbuf[1 - cur, pl.ds(r0, L), :], ws_ref[:, p0:p0 + proj_w],
                preferred_element_type=F32).astype(BF16)

            n0 = g * n_state
            bg = b_s[:, n0:n0 + n_state]
            cg = c_s[:, n0:n0 + n_state]
            cb = lax.dot_general(cg.astype(BF16), bg, (((1,), (1,)), ((), ())),
                                 preferred_element_type=F32)
            cb = jnp.where(causal, cb, 0.0)
            elast_tiles = []
            for jp in range(HEADS_PER_GROUP // 2):
                col0 = (g * (HEADS_PER_GROUP // 2) + jp) * LANES
                lhs = []
                cols = []
                for e in range(2):
                    h = g * HEADS_PER_GROUP + jp * 2 + e
                    col_a = acum_s[:, h:h + 1]
                    row_a = acumt_s[h:h + 1, :]
                    last = acum_s[L - 1:L, h:h + 1]
                    seg = col_a - row_a
                    decay = jnp.exp(jnp.minimum(seg, 0.0))
                    m_mat = (cb * decay).astype(BF16)
                    c_scaled = (cg * jnp.exp(col_a)).astype(BF16)
                    lhs.append(jnp.concatenate([m_mat, c_scaled], axis=1))
                    cols.append((dt_s[:, h:h + 1], jnp.exp(last - col_a), jnp.exp(last)))
                dt_pair = jnp.where(low_half, cols[0][0], cols[1][0])
                w_pair = jnp.where(low_half, cols[0][1], cols[1][1])
                elast_tiles.append(jnp.where(low_half[0:1, :], cols[0][2], cols[1][2]))
                xs_pair = xs_s[:, col0:col0 + LANES]
                xdt = xs_pair * dt_pair
                xsc_s[:, col0:col0 + LANES] = (xdt * w_pair).astype(BF16)
                rhs = jnp.concatenate([xdt.astype(BF16), state[:, col0:col0 + LANES].astype(BF16)],
                                      axis=0)
                y0 = jnp.dot(lhs[0], rhs, preferred_element_type=F32)
                y1 = jnp.dot(lhs[1], rhs, preferred_element_type=F32)
                y_s[:, col0:col0 + LANES] = (jnp.where(low_half, y0, y1)
                                             + dskip_ref[:, col0:col0 + LANES] * xs_pair)
            g0 = g * HEADS_PER_GROUP * SSM_HEAD_DIM
            gw = HEADS_PER_GROUP * SSM_HEAD_DIM
            upd = lax.dot_general(bg, xsc_s[:, g0:g0 + gw], (((0,), (0,)), ((), ())),
                                  preferred_element_type=F32)
            elast = jnp.concatenate(elast_tiles, axis=1)
            state[:, g0:g0 + gw] = state[:, g0:g0 + gw] * elast + upd

        ssq = jnp.zeros((L, 1), F32)
        for hf in range(2):
            z = zs_ref[hf, pl.ds(r0, L), :].astype(F32)
            gy = y_s[:, hf * CHUNK_W:(hf + 1) * CHUNK_W] * _silu(z)
            y_s[:, hf * CHUNK_W:(hf + 1) * CHUNK_W] = gy
            ssq = ssq + jnp.sum(gy * gy, axis=-1, keepdims=True)
        inv = lax.rsqrt(ssq * (1.0 / SSM_INNER) + RMS_EPS)
        ynbuf[cur, pl.ds(r0, L), :] = (y_s[...] * inv * normw_ref[...]).astype(BF16)
        return carry

    lax.fori_loop(0, ts // L, chunk_body, 0)


def _ssd(proj, dt_raw, conv_w, conv_b, dt_bias, a_log, d_skip, ssm_norm_w, w_s, batch, seq):
    t = batch * seq
    ts = min(256, seq)
    steps = seq // ts
    n_steps = batch * steps
    last = n_steps - 1
    pad = LANES - SSM_HEADS
    dtb = jnp.pad(dt_bias.astype(F32), (0, pad)).reshape(1, LANES)
    alog = jnp.pad(a_log.astype(F32), (0, pad)).reshape(1, LANES)
    dskip = jnp.repeat(d_skip.astype(F32), SSM_HEAD_DIM).reshape(1, SSM_INNER)
    tril = jnp.tril(jnp.ones((SSM_CHUNK, SSM_CHUNK), F32))

    def full(shape, **kw):
        return pl.BlockSpec(shape, lambda g: (0,) * len(shape), **kw)

    def rows_of(g):
        return jnp.minimum(g, last)

    return pl.pallas_call(
        functools.partial(_ssd_kernel, ts=ts, steps=steps),
        grid=(n_steps + 1,),
        in_specs=[
            pl.BlockSpec((2, ts, CHUNK_W), lambda g: (CH_XS // 2, rows_of(g), 0)),
            pl.BlockSpec((None, ts, CHUNK_W), lambda g: (CH_BC, rows_of(g), 0)),
            pl.BlockSpec((2, ts, CHUNK_W), lambda g: (CH_ZS // 2, rows_of(g), 0)),
            pl.BlockSpec((ts, LANES), lambda g: (rows_of(g), 0)),
            full((SSM_CONV, SSM_CONV_DIM)),
            full((1, SSM_CONV_DIM)),
            full((1, LANES)),
            full((1, LANES)),
            full((1, SSM_INNER)),
            full((1, SSM_INNER)),
            full((SSM_CHUNK, SSM_CHUNK)),
            full((SSM_INNER, D_MODEL), pipeline_mode=pl.Buffered(1)),
        ],
        out_specs=pl.BlockSpec((ts, D_MODEL), lambda g: (jnp.maximum(g - 1, 0), 0)),
        out_shape=jax.ShapeDtypeStruct((t, D_MODEL), BF16),
        scratch_shapes=[
            pltpu.VMEM((SSM_CONV_DIM // LANES, SSM_CHUNK + 8, LANES), F32),
            pltpu.VMEM((SSM_STATE, SSM_INNER), F32),
            pltpu.VMEM((SSM_CHUNK, SSM_INNER), F32),
            pltpu.VMEM((SSM_CHUNK, SSM_GROUPS * SSM_STATE), BF16),
            pltpu.VMEM((SSM_CHUNK, SSM_GROUPS * SSM_STATE), F32),
            pltpu.VMEM((SSM_CHUNK, LANES), F32),
            pltpu.VMEM((LANES, SSM_CHUNK), F32),
            pltpu.VMEM((SSM_CHUNK, LANES), F32),
            pltpu.VMEM((SSM_CHUNK, SSM_INNER), BF16),
            pltpu.VMEM((SSM_CHUNK, SSM_INNER), F32),
            pltpu.VMEM((2, ts, SSM_INNER), BF16),
        ],
        compiler_params=_compiler_params(1, 56),
        name="ssd",
    )(proj, proj, proj, dt_raw, conv_w.astype(F32), conv_b.astype(F32).reshape(1, SSM_CONV_DIM),
      dtb, alog, dskip, ssm_norm_w.astype(F32).reshape(1, SSM_INNER), tril, w_s)


def _tail_kernel(oa_ref, ys_ref, ga_ref, gs_ref, x_ref, wa_ref, wo_ref, nw_ref, o_ref):
    a = jnp.dot(oa_ref[...], wa_ref[...], preferred_element_type=F32)
    ga = _sigmoid(ga_ref[...].astype(F32))
    gs = _sigmoid(gs_ref[...].astype(F32))
    merged = (ga * a + gs * ys_ref[...].astype(F32)).astype(BF16)
    r = x_ref[...] + jnp.dot(merged, wo_ref[...], preferred_element_type=F32)
    ms = jnp.mean(r * r, axis=-1, keepdims=True)
    o_ref[...] = r * lax.rsqrt(ms + RMS_EPS) * nw_ref[...]


def _tail(o_a, y_s, proj, x2, w_a, w_out, final_norm_w):
    t = x2.shape[0]
    tm = min(256, t)

    def rows(shape):
        return pl.BlockSpec(shape, lambda i: (i, 0))

    def resident(shape):
        return pl.BlockSpec(shape, lambda i: (0, 0), pipeline_mode=pl.Buffered(1))

    return pl.pallas_call(
        _tail_kernel,
        grid=(t // tm,),
        in_specs=[
            rows((tm, ATTN_WIDTH)),
            rows((tm, D_MODEL)),
            pl.BlockSpec((None, tm, CHUNK_W), lambda i: (CH_GA, i, 0)),
            pl.BlockSpec((None, tm, CHUNK_W), lambda i: (CH_GS, i, 0)),
            rows((tm, D_MODEL)),
            resident((ATTN_WIDTH, D_MODEL)),
            resident((D_MODEL, D_MODEL)),
            resident((1, D_MODEL)),
        ],
        out_specs=rows((tm, D_MODEL)),
        out_shape=jax.ShapeDtypeStruct((t, D_MODEL), F32),
        compiler_params=_compiler_params(1, 48),
        name="tail",
    )(o_a, y_s, proj, proj, x2, w_a, w_out, final_norm_w)


def _layer(x, norm_w, w_in, conv_w, conv_b, dt_bias, a_log, d_skip, ssm_norm_w,
           w_attn_branch, w_ssm_branch, w_out, out_norm_w):
    batch, seq, _ = x.shape
    t = batch * seq
    x2 = x.reshape(t, D_MODEL)

    w_all = w_in.astype(BF16)
    w_gates = w_all[:, DT_COL0 + SSM_HEADS:]
    w_dt = jnp.pad(w_all[:, DT_COL0:DT_COL0 + SSM_HEADS], ((0, 0), (0, LANES - SSM_HEADS)))
    proj, dt_raw = _in_proj(x2, norm_w.astype(F32).reshape(1, D_MODEL), w_all, w_gates, w_dt)

    o_a = _attention(proj, batch, seq)
    y_s = _ssd(proj, dt_raw, conv_w, conv_b, dt_bias, a_log, d_skip, ssm_norm_w,
               w_ssm_branch.astype(BF16), batch, seq)
    out = _tail(o_a, y_s, proj, x2, w_attn_branch.astype(BF16), w_out.astype(BF16),
                out_norm_w.astype(F32).reshape(1, D_MODEL))
    return out.reshape(batch, seq, D_MODEL)


def kernel(x, norm_w, w_in, conv_w, conv_b, dt_bias, a_log, d_skip, ssm_norm_w,
           w_attn_branch, w_ssm_branch, w_out, final_norm_w):
    depth = norm_w.shape[0]
    assert depth == 1, "the residual stream between layers is not normalised; only depth 1 is fused"
    assert x.shape[1] % ATTN_ROWS == 0 and x.shape[2] == D_MODEL
    return _layer(x, norm_w[0], w_in[0], conv_w[0], conv_b[0], dt_bias[0], a_log[0], d_skip[0],
                  ssm_norm_w[0], w_attn_branch[0], w_ssm_branch[0], w_out[0], final_norm_w)
```

```python
import functools

import jax
import jax.numpy as jnp
from jax import lax
from jax.experimental import pallas as pl
from jax.experimental.pallas import tpu as pltpu

D_MODEL = 2048
ATTN_HEADS = 16
HEAD_DIM = 128
ATTN_WIDTH = ATTN_HEADS * HEAD_DIM
PATTERNS = ((128, 1), (512, 4), (2048, 16))
ATTN_BLOCK = 128

SSM_INNER = 4096
SSM_HEAD_DIM = 64
SSM_HEADS = 64
SSM_GROUPS = 8
SSM_STATE = 128
SSM_CONV = 4
SSM_CHUNK = 128
SSM_CONV_DIM = SSM_INNER + 2 * SSM_GROUPS * SSM_STATE
HEADS_PER_GROUP = SSM_HEADS // SSM_GROUPS
RMS_EPS = 1e-6
LOG2E = 1.4426950408889634

LANES = 128
CHUNK_W = D_MODEL
CH_Q, CH_K, CH_V, CH_ZA, CH_ZS, CH_XS, CH_BC, CH_GA, CH_GS = 0, 1, 2, 3, 4, 6, 8, 9, 10
N_CHUNKS = 11
DT_COL0 = 4 * ATTN_WIDTH + SSM_INNER + SSM_CONV_DIM

F32 = jnp.float32
BF16 = jnp.bfloat16
MIB = 1024 * 1024


def _sigmoid(v):
    return 0.5 + 0.5 * jnp.tanh(0.5 * v)


def _silu(v):
    h = 0.5 * v
    return h + h * jnp.tanh(h)


def _compiler_params(n_grid, vmem_mib):
    return pltpu.CompilerParams(
        dimension_semantics=("arbitrary",) * n_grid,
        vmem_limit_bytes=vmem_mib * MIB,
    )


def _in_proj_kernel(x_ref, nw_ref, w_ref, wg_ref, wdt_ref, o_ref, dt_ref, hn_ref, *, n_lead):
    j = pl.program_id(1)

    @pl.when(j == 0)
    def _():
        x = x_ref[...]
        ms = jnp.mean(x * x, axis=-1, keepdims=True)
        hn = (x * lax.rsqrt(ms + RMS_EPS) * nw_ref[...]).astype(BF16)
        hn_ref[...] = hn
        dt_ref[...] = jnp.dot(hn, wdt_ref[...], preferred_element_type=F32)

    @pl.when(j < n_lead)
    def _():
        o_ref[...] = jnp.dot(hn_ref[...], w_ref[...], preferred_element_type=F32).astype(BF16)

    @pl.when(j >= n_lead)
    def _():
        o_ref[...] = jnp.dot(hn_ref[...], wg_ref[...], preferred_element_type=F32).astype(BF16)


def _in_proj(x2, norm_w, w_all, w_gates, w_dt):
    t = x2.shape[0]
    tm = min(1024, t)
    tn = 1024
    per = CHUNK_W // tn
    n_lead = DT_COL0 // tn
    return pl.pallas_call(
        functools.partial(_in_proj_kernel, n_lead=n_lead),
        grid=(t // tm, N_CHUNKS * per),
        in_specs=[
            pl.BlockSpec((tm, D_MODEL), lambda i, j: (i, 0)),
            pl.BlockSpec((1, D_MODEL), lambda i, j: (0, 0)),
            pl.BlockSpec((D_MODEL, tn), lambda i, j: (0, jnp.minimum(j, n_lead - 1))),
            pl.BlockSpec((D_MODEL, tn), lambda i, j: (0, jnp.maximum(j - n_lead, 0))),
            pl.BlockSpec((D_MODEL, LANES), lambda i, j: (0, 0)),
        ],
        out_specs=[
            pl.BlockSpec((None, tm, tn), lambda i, j: (j // per, i, j % per)),
            pl.BlockSpec((tm, LANES), lambda i, j: (i, 0)),
        ],
        out_shape=[
            jax.ShapeDtypeStruct((N_CHUNKS, t, CHUNK_W), BF16),
            jax.ShapeDtypeStruct((t, LANES), F32),
        ],
        scratch_shapes=[pltpu.VMEM((tm, D_MODEL), BF16)],
        compiler_params=_compiler_params(2, 48),
        name="in_proj",
    )(x2, norm_w, w_all, w_gates, w_dt)


ATTN_ROWS = PATTERNS[-1][1] * ATTN_BLOCK
ATTN_HEADS_PER_STEP = 2
ATTN_STEP_W = ATTN_HEADS_PER_STEP * HEAD_DIM
ATTN_UNROLL = 16


def _attn_kernel(q_ref, k_ref, v_ref, z_ref, dist_ref, dist0_ref, slope_ref, o_ref,
                 slabs, kd, vd, qd, oacc, lacc, bias):
    i = pl.program_id(2)
    blk = ATTN_BLOCK
    rows = ATTN_ROWS
    nh = ATTN_HEADS_PER_STEP
    scale = HEAD_DIM ** -0.5

    geo = []
    kv_base = q_base = 0
    for _, dil in PATTERNS:
        sub = rows // dil
        geo.append((dil, sub, kv_base, sub + blk, q_base))
        kv_base += dil * (sub + blk)
        q_base += rows if dil > 1 else 0

    prev_rows = [(base + r * pitch, sub) for dil, sub, base, pitch, _ in geo for r in range(dil)]

    @pl.when(i == 0)
    def _():
        for p0, _ in prev_rows:
            kd[p0:p0 + blk, :] = jnp.zeros((blk, ATTN_STEP_W), BF16)
            vd[p0:p0 + blk, :] = jnp.zeros((blk, ATTN_STEP_W), BF16)

    @pl.when(i != 0)
    def _():
        for p0, sub in prev_rows:
            kd[p0:p0 + blk, :] = kd[p0 + sub:p0 + sub + blk, :]
            vd[p0:p0 + blk, :] = vd[p0 + sub:p0 + sub + blk, :]

    def regroup(src_ref, dst, is_kv):
        def stage(rb, carry):
            rws = pl.ds(pl.multiple_of(rb * blk, blk), blk)
            for s in range(nh):
                slabs[0, s, rws, :] = src_ref[rws, s * LANES:(s + 1) * LANES].astype(F32)
            return carry

        lax.fori_loop(0, rows // blk, stage, 0, unroll=True)
        level, d_prev, sub_prev = 0, 1, rows
        for li, (dil, sub, base, pitch, qb) in enumerate(geo):
            nblk = sub // blk
            if dil == 1:
                if is_kv:
                    def copy(rb, carry, base=base):
                        r0 = pl.multiple_of(rb * blk, blk)
                        dst[pl.ds(base + blk + r0, blk), :] = src_ref[pl.ds(r0, blk), :]
                        return carry

                    lax.fori_loop(0, nblk, copy, 0, unroll=True)
                continue
            ratio = dil // d_prev
            keep = li < len(geo) - 1

            def gather(it, carry, sub=sub, base=base, pitch=pitch, qb=qb, nblk=nblk,
                       ratio=ratio, d_prev=d_prev, sub_prev=sub_prev, level=level, keep=keep):
                r = it // nblk
                b = it % nblk
                src0 = (r % d_prev) * sub_prev + r // d_prev + b * blk * ratio
                p0 = base + r * pitch + blk if is_kv else qb + r * sub
                p0 = pl.multiple_of(p0 + b * blk, blk)
                for s in range(nh):
                    val = slabs[level, s, pl.ds(src0, blk, stride=ratio), :]
                    dst[pl.ds(p0, blk), s * LANES:(s + 1) * LANES] = val.astype(BF16)
                    if keep:
                        slabs[1 - level, s, pl.ds(pl.multiple_of(r * sub + b * blk, blk), blk), :] = val
                return carry

            lax.fori_loop(0, dil * nblk, gather, 0, unroll=True)
            level, d_prev, sub_prev = 1 - level, dil, sub

    regroup(k_ref, kd, True)
    regroup(v_ref, vd, True)
    regroup(q_ref, qd, False)

    for pi, (dil, _, _, _, _) in enumerate(geo):
        for s in range(nh):
            slope = slope_ref[s] * (float(dil) * LOG2E)
            bias[(pi * nh + s) * 2] = slope * dist_ref[...]
            bias[(pi * nh + s) * 2 + 1] = slope * dist0_ref[...]

    ones = jnp.ones((2 * blk, LANES), BF16)

    def unit(q, kc, vc, bias_t):
        s = lax.dot_general(q, kc, (((1,), (1,)), ((), ())), preferred_element_type=F32)
        s = s * (scale * LOG2E) - bias_t
        m = jnp.max(s, axis=-1, keepdims=True)
        p = jnp.exp2(s - m)
        pv = jnp.dot(p.astype(BF16), jnp.concatenate([vc, ones], axis=1),
                     preferred_element_type=F32)
        l_b = pv[:, LANES:]
        m_b = jnp.broadcast_to(m, (blk, LANES))
        return pv[:, :LANES] * (1.0 / l_b), m_b + jnp.log(l_b) * LOG2E

    for pi, (dil, sub, base, pitch, qb) in enumerate(geo):
        nblk = sub // blk

        def body(it, carry, dil=dil, sub=sub, base=base, pitch=pitch, qb=qb, nblk=nblk, pi=pi):
            r = it // nblk
            b = it % nblk
            no_prev = jnp.logical_and(i == 0, b == 0).astype(jnp.int32)
            q0 = pl.multiple_of(qb + r * sub + b * blk, blk)
            k0 = pl.multiple_of(base + r * pitch + b * blk, blk)
            t0 = r + b * blk * dil
            if dil == 1:
                tok = pl.ds(pl.multiple_of(t0, blk), blk)
            else:
                tok = pl.ds(t0, blk, stride=dil)
            for s in range(nh):
                c0 = s * LANES
                q_src = q_ref if dil == 1 else qd
                o, lse = unit(q_src[pl.ds(q0, blk), c0:c0 + LANES],
                              kd[pl.ds(k0, 2 * blk), c0:c0 + LANES],
                              vd[pl.ds(k0, 2 * blk), c0:c0 + LANES],
                              bias[(pi * nh + s) * 2 + no_prev])
                oacc[pi * nh + s, tok, :] = o
                lacc[pi * nh + s, tok, :] = lse
            return carry

        lax.fori_loop(0, dil * nblk, body, 0, unroll=ATTN_UNROLL)

    n_pat = len(geo)

    def mix(bi, carry):
        tok = pl.ds(pl.multiple_of(bi * blk, blk), blk)
        for s in range(nh):
            lses = [lacc[pi * nh + s, tok, :] for pi in range(n_pat)]
            top = functools.reduce(jnp.maximum, lses)
            ws = [jnp.exp2(l - top) for l in lses]
            num = sum(w * oacc[pi * nh + s, tok, :] for pi, w in enumerate(ws))
            z = z_ref[tok, s * LANES:(s + 1) * LANES].astype(F32)
            o_ref[tok, s * LANES:(s + 1) * LANES] = (
                num * (1.0 / sum(ws)) * _silu(z)).astype(BF16)
        return carry

    lax.fori_loop(0, rows // blk, mix, 0, unroll=4)


def _dist_tables():
    blk = ATTN_BLOCK
    qi = jnp.arange(blk)[:, None]
    ki = jnp.arange(2 * blk)[None, :]
    dist = qi - ki + blk
    valid = (dist >= 0) & (dist <= blk)
    d = jnp.where(valid, dist.astype(F32), jnp.inf)
    d0 = jnp.where(ki >= blk, d, jnp.inf)
    return d, d0


def _attention(proj, batch, seq):
    assert all(w // d == ATTN_BLOCK for w, d in PATTERNS)
    t = batch * seq
    rows = ATTN_ROWS
    steps = seq // rows
    n_groups = ATTN_HEADS // ATTN_HEADS_PER_STEP
    slopes = jnp.asarray([2.0 ** (-8.0 * (h + 1) / ATTN_HEADS) for h in range(ATTN_HEADS)], F32)
    slope_tab = jnp.broadcast_to(slopes[:, None, None], (ATTN_HEADS, 1, 2 * ATTN_BLOCK))
    dist, dist0 = _dist_tables()
    kv_rows = sum(d * (rows // d + ATTN_BLOCK) for _, d in PATTERNS)
    q_rows = sum(rows for _, d in PATTERNS if d > 1)

    def chunk_spec(ch):
        return pl.BlockSpec((None, rows, ATTN_STEP_W), lambda b, g, i: (ch, b * steps + i, g))

    tab_spec = pl.BlockSpec((ATTN_BLOCK, 2 * ATTN_BLOCK), lambda b, g, i: (0, 0))
    slope_spec = pl.BlockSpec((ATTN_HEADS_PER_STEP, 1, 2 * ATTN_BLOCK), lambda b, g, i: (g, 0, 0))
    return pl.pallas_call(
        _attn_kernel,
        grid=(batch, n_groups, steps),
        in_specs=[chunk_spec(CH_Q), chunk_spec(CH_K), chunk_spec(CH_V), chunk_spec(CH_ZA),
                  tab_spec, tab_spec, slope_spec],
        out_specs=pl.BlockSpec((rows, ATTN_STEP_W), lambda b, g, i: (b * steps + i, g)),
        out_shape=jax.ShapeDtypeStruct((t, ATTN_WIDTH), BF16),
        scratch_shapes=[
            pltpu.VMEM((2, ATTN_HEADS_PER_STEP, rows, LANES), F32),
            pltpu.VMEM((kv_rows, ATTN_STEP_W), BF16),
            pltpu.VMEM((kv_rows, ATTN_STEP_W), BF16),
            pltpu.VMEM((q_rows, ATTN_STEP_W), BF16),
            pltpu.VMEM((len(PATTERNS) * ATTN_HEADS_PER_STEP, rows, LANES), F32),
            pltpu.VMEM((len(PATTERNS) * ATTN_HEADS_PER_STEP, rows, LANES), F32),
            pltpu.VMEM((len(PATTERNS) * ATTN_HEADS_PER_STEP * 2, ATTN_BLOCK, 2 * ATTN_BLOCK), F32),
        ],
        compiler_params=_compiler_params(3, 48),
        name="attn",
    )(proj, proj, proj, proj, dist, dist0, slope_tab)


def _ssd_kernel(xs_ref, bc_ref, zs_ref, dt_ref, convw_ref, convb_ref, dtb_ref, alog_ref,
                dskip_ref, normw_ref, tril_ref, ws_ref, y_ref,
                xext, state, xs_s, b_s, c_s, acum_s, acumt_s, dt_s, xsc_s, y_s, ynbuf,
                *, ts, steps):
    L = SSM_CHUNK
    n_state = SSM_STATE
    n_slabs = SSM_CONV_DIM // LANES
    g_step = pl.program_id(0)
    cur = g_step % 2

    @pl.when(g_step == 0)
    def _():
        ynbuf[1] = jnp.zeros((ts, SSM_INNER), BF16)

    @pl.when(g_step % steps == 0)
    def _():
        state[...] = jnp.zeros_like(state)
        for s in range(n_slabs):
            xext[s, 0:8, :] = jnp.zeros((8, LANES), F32)

    row = lax.broadcasted_iota(jnp.int32, (L, L), 0)
    col = lax.broadcasted_iota(jnp.int32, (L, L), 1)
    causal = row >= col
    low_half = col < SSM_HEAD_DIM
    neg_a = -jnp.exp(alog_ref[...])

    def chunk_body(c, carry):
        r0 = pl.multiple_of(c * L, L)

        per_chunk = CHUNK_W // LANES
        for s in range(n_slabs):
            lo = s * LANES
            c0 = (s % per_chunk) * LANES
            if s < 2 * per_chunk:
                raw = xs_ref[s // per_chunk, pl.ds(r0, L), c0:c0 + LANES]
            else:
                raw = bc_ref[pl.ds(r0, L), c0:c0 + LANES]
            xext[s, 8:8 + L, :] = raw.astype(F32)
            acc = jnp.broadcast_to(convb_ref[:, lo:lo + LANES], (L, LANES))
            for k in range(SSM_CONV):
                shift = 8 - (SSM_CONV - 1) + k
                acc = acc + convw_ref[k:k + 1, lo:lo + LANES] * xext[s, shift:shift + L, :]
            xext[s, 0:8, :] = xext[s, L:L + 8, :]
            xc = _silu(acc)
            if lo < SSM_INNER:
                xs_s[:, lo:lo + LANES] = xc
            elif lo < SSM_INNER + SSM_GROUPS * n_state:
                b_s[:, lo - SSM_INNER:lo - SSM_INNER + LANES] = xc.astype(BF16)
            else:
                o2 = lo - SSM_INNER - SSM_GROUPS * n_state
                c_s[:, o2:o2 + LANES] = xc

        dt_in = dt_ref[pl.ds(r0, L), :] + dtb_ref[...]
        dtv = jnp.maximum(dt_in, 0.0) + jnp.log(1.0 + jnp.exp(-jnp.abs(dt_in)))
        da = dtv * neg_a
        acum = jnp.dot(tril_ref[...], da, preferred_element_type=F32,
                       precision=lax.Precision.HIGHEST)
        acum_s[...] = acum
        acumt_s[...] = acum.T
        dt_s[...] = dtv

        proj_w = D_MODEL // SSM_GROUPS
        for g in range(SSM_GROUPS):
            p0 = g * proj_w
            y_ref[pl.ds(r0, L), p0:p0 + proj_w] = jnp.dot(
                ynbuf[1 - cur, pl.ds(r0, L), :], ws_ref[:, p0:p0 + proj_w],
                preferred_element_type=F32).astype(BF16)

            n0 = g * n_state
            bg = b_s[:, n0:n0 + n_state]
            cg = c_s[:, n0:n0 + n_state]
            cb = lax.dot_general(cg.astype(BF16), bg, (((1,), (1,)), ((), ())),
                                 preferred_element_type=F32)
            cb = jnp.where(causal, cb, 0.0)
            elast_tiles = []
            for jp in range(HEADS_PER_GROUP // 2):
                col0 = (g * (HEADS_PER_GROUP // 2) + jp) * LANES
                lhs = []
                cols = []
                for e in range(2):
                    h = g * HEADS_PER_GROUP + jp * 2 + e
                    col_a = acum_s[:, h:h + 1]
                    row_a = acumt_s[h:h + 1, :]
                    last = acum_s[L - 1:L, h:h + 1]
                    seg = col_a - row_a
                    decay = jnp.exp(jnp.minimum(seg, 0.0))
                    m_mat = (cb * decay).astype(BF16)
                    c_scaled = (cg * jnp.exp(col_a)).astype(BF16)
                    lhs.append(jnp.concatenate([m_mat, c_scaled], axis=1))
                    cols.append((dt_s[:, h:h + 1], jnp.exp(last - col_a), jnp.exp(last)))
                dt_pair = jnp.where(low_half, cols[0][0], cols[1][0])
                w_pair = jnp.where(low_half, cols[0][1], cols[1][1])
                elast_tiles.append(jnp.where(low_half[0:1, :], cols[0][2], cols[1][2]))
                xs_pair = xs_s[:, col0:col0 + LANES]
                xdt = xs_pair * dt_pair
                xsc_s[:, col0:col0 + LANES] = (xdt * w_pair).astype(BF16)
                rhs = jnp.concatenate([xdt.astype(BF16), state[:, col0:col0 + LANES].astype(BF16)],
                                      axis=0)
                y0 = jnp.dot(lhs[0], rhs, preferred_element_type=F32)
                y1 = jnp.dot(lhs[1], rhs, preferred_element_type=F32)
                y_s[:, col0:col0 + LANES] = (jnp.where(low_half, y0, y1)
                                             + dskip_ref[:, col0:col0 + LANES] * xs_pair)
            g0 = g * HEADS_PER_GROUP * SSM_HEAD_DIM
            gw = HEADS_PER_GROUP * SSM_HEAD_DIM
            upd = lax.dot_general(bg, xsc_s[:, g0:g0 + gw], (((0,), (0,)), ((), ())),
                                  preferred_element_type=F32)
            elast = jnp.concatenate(elast_tiles, axis=1)
            state[:, g0:g0 + gw] = state[:, g0:g0 + gw] * elast + upd

        piece = 4 * LANES
        ssq = jnp.zeros((L, 1), F32)
        for pc in range(SSM_INNER // piece):
            lo = pc * piece
            z = zs_ref[lo // CHUNK_W, pl.ds(r0, L), lo % CHUNK_W:lo % CHUNK_W + piece].astype(F32)
            gy = y_s[:, lo:lo + piece] * _silu(z)
            y_s[:, lo:lo + piece] = gy
            ssq = ssq + jnp.sum(gy * gy, axis=-1, keepdims=True)
        inv = lax.rsqrt(ssq * (1.0 / SSM_INNER) + RMS_EPS)
        for pc in range(SSM_INNER // piece):
            lo = pc * piece
            ynbuf[cur, pl.ds(r0, L), lo:lo + piece] = (
                y_s[:, lo:lo + piece] * inv * normw_ref[:, lo:lo + piece]).astype(BF16)
        return carry

    lax.fori_loop(0, ts // L, chunk_body, 0)


def _ssd(proj, dt_raw, conv_w, conv_b, dt_bias, a_log, d_skip, ssm_norm_w, w_s, batch, seq):
    t = batch * seq
    ts = min(256, seq)
    steps = seq // ts
    n_steps = batch * steps
    last = n_steps - 1
    pad = LANES - SSM_HEADS
    dtb = jnp.pad(dt_bias.astype(F32), (0, pad)).reshape(1, LANES)
    alog = jnp.pad(a_log.astype(F32), (0, pad)).reshape(1, LANES)
    dskip = jnp.repeat(d_skip.astype(F32), SSM_HEAD_DIM).reshape(1, SSM_INNER)
    tril = jnp.tril(jnp.ones((SSM_CHUNK, SSM_CHUNK), F32))

    def full(shape, **kw):
        return pl.BlockSpec(shape, lambda g: (0,) * len(shape), **kw)

    def rows_of(g):
        return jnp.minimum(g, last)

    return pl.pallas_call(
        functools.partial(_ssd_kernel, ts=ts, steps=steps),
        grid=(n_steps + 1,),
        in_specs=[
            pl.BlockSpec((2, ts, CHUNK_W), lambda g: (CH_XS // 2, rows_of(g), 0)),
            pl.BlockSpec((None, ts, CHUNK_W), lambda g: (CH_BC, rows_of(g), 0)),
            pl.BlockSpec((2, ts, CHUNK_W), lambda g: (CH_ZS // 2, rows_of(g), 0)),
            pl.BlockSpec((ts, LANES), lambda g: (rows_of(g), 0)),
            full((SSM_CONV, SSM_CONV_DIM)),
            full((1, SSM_CONV_DIM)),
            full((1, LANES)),
            full((1, LANES)),
            full((1, SSM_INNER)),
            full((1, SSM_INNER)),
            full((SSM_CHUNK, SSM_CHUNK)),
            full((SSM_INNER, D_MODEL), pipeline_mode=pl.Buffered(1)),
        ],
        out_specs=pl.BlockSpec((ts, D_MODEL), lambda g: (jnp.maximum(g - 1, 0), 0)),
        out_shape=jax.ShapeDtypeStruct((t, D_MODEL), BF16),
        scratch_shapes=[
            pltpu.VMEM((SSM_CONV_DIM // LANES, SSM_CHUNK + 8, LANES), F32),
            pltpu.VMEM((SSM_STATE, SSM_INNER), F32),
            pltpu.VMEM((SSM_CHUNK, SSM_INNER), F32),
            pltpu.VMEM((SSM_CHUNK, SSM_GROUPS * SSM_STATE), BF16),
            pltpu.VMEM((SSM_CHUNK, SSM_GROUPS * SSM_STATE), F32),
            pltpu.VMEM((SSM_CHUNK, LANES), F32),
            pltpu.VMEM((LANES, SSM_CHUNK), F32),
            pltpu.VMEM((SSM_CHUNK, LANES), F32),
            pltpu.VMEM((SSM_CHUNK, SSM_INNER), BF16),
            pltpu.VMEM((SSM_CHUNK, SSM_INNER), F32),
            pltpu.VMEM((2, ts, SSM_INNER), BF16),
        ],
        compiler_params=_compiler_params(1, 56),
        name="ssd",
    )(proj, proj, proj, dt_raw, conv_w.astype(F32), conv_b.astype(F32).reshape(1, SSM_CONV_DIM),
      dtb, alog, dskip, ssm_norm_w.astype(F32).reshape(1, SSM_INNER), tril, w_s)


def _tail_kernel(oa_ref, ys_ref, ga_ref, gs_ref, x_ref, wa_ref, wo_ref, nw_ref, o_ref):
    a = jnp.dot(oa_ref[...], wa_ref[...], preferred_element_type=F32)
    ga = _sigmoid(ga_ref[...].astype(F32))
    gs = _sigmoid(gs_ref[...].astype(F32))
    merged = (ga * a + gs * ys_ref[...].astype(F32)).astype(BF16)
    r = x_ref[...] + jnp.dot(merged, wo_ref[...], preferred_element_type=F32)
    ms = jnp.mean(r * r, axis=-1, keepdims=True)
    o_ref[...] = r * lax.rsqrt(ms + RMS_EPS) * nw_ref[...]


def _tail(o_a, y_s, proj, x2, w_a, w_out, final_norm_w):
    t = x2.shape[0]
    tm = min(256, t)

    def rows(shape):
        return pl.BlockSpec(shape, lambda i: (i, 0))

    def resident(shape):
        return pl.BlockSpec(shape, lambda i: (0, 0), pipeline_mode=pl.Buffered(1))

    return pl.pallas_call(
        _tail_kernel,
        grid=(t // tm,),
        in_specs=[
            rows((tm, ATTN_WIDTH)),
            rows((tm, D_MODEL)),
            pl.BlockSpec((None, tm, CHUNK_W), lambda i: (CH_GA, i, 0)),
            pl.BlockSpec((None, tm, CHUNK_W), lambda i: (CH_GS, i, 0)),
            rows((tm, D_MODEL)),
            resident((ATTN_WIDTH, D_MODEL)),
            resident((D_MODEL, D_MODEL)),
            resident((1, D_MODEL)),
        ],
        out_specs=rows((tm, D_MODEL)),
        out_shape=jax.ShapeDtypeStruct((t, D_MODEL), F32),
        compiler_params=_compiler_params(1, 48),
        name="tail",
    )(o_a, y_s, proj, proj, x2, w_a, w_out, final_norm_w)


def _layer(x, norm_w, w_in, conv_w, conv_b, dt_bias, a_log, d_skip, ssm_norm_w,
           w_attn_branch, w_ssm_branch, w_out, out_norm_w):
    batch, seq, _ = x.shape
    t = batch * seq
    x2 = x.reshape(t, D_MODEL)

    w_all = w_in.astype(BF16)
    w_gates = w_all[:, DT_COL0 + SSM_HEADS:]
    w_dt = jnp.pad(w_all[:, DT_COL0:DT_COL0 + SSM_HEADS], ((0, 0), (0, LANES - SSM_HEADS)))
    proj, dt_raw = _in_proj(x2, norm_w.astype(F32).reshape(1, D_MODEL), w_all, w_gates, w_dt)

    o_a = _attention(proj, batch, seq)
    y_s = _ssd(proj, dt_raw, conv_w, conv_b, dt_bias, a_log, d_skip, ssm_norm_w,
               w_ssm_branch.astype(BF16), batch, seq)
    out = _tail(o_a, y_s, proj, x2, w_attn_branch.astype(BF16), w_out.astype(BF16),
                out_norm_w.astype(F32).reshape(1, D_MODEL))
    return out.reshape(batch, seq, D_MODEL)


def kernel(x, norm_w, w_in, conv_w, conv_b, dt_bias, a_log, d_skip, ssm_norm_w,
           w_attn_branch, w_ssm_branch, w_out, final_norm_w):
    depth = norm_w.shape[0]
    assert depth == 1, "the residual stream between layers is not normalised; only depth 1 is fused"
    assert x.shape[1] % ATTN_ROWS == 0 and x.shape[2] == D_MODEL
    return _layer(x, norm_w[0], w_in[0], conv_w[0], conv_b[0], dt_bias[0], a_log[0], d_skip[0],
                  ssm_norm_w[0], w_attn_branch[0], w_ssm_branch[0], w_out[0], final_norm_w)
```
